```python
import math
import jax, jax.numpy as jnp
from jax import lax
import numpy as np

D_MODEL = 1024
BATCH = 4
SEQ = 4096
DEPTH = 4
DEC_BATCH = 128
DEC_SEQ = 1
PAST_LEN = 2048
PAGE_SIZE = 128

N_EVEN = (DEPTH + 1) // 2
N_ODD = DEPTH // 2
MIX_HALF = D_MODEL // 2
EPS = 1e-6

GLA_HEADS = 4
GLA_DV = MIX_HALF // GLA_HEADS
GLA_DK = GLA_DV // 2
GLA_LR = 16
GLA_GATE_NORM = 16.0
GLA_CHUNK = 64
GLA_COLS = 2 * GLA_HEADS * GLA_DK + 2 * MIX_HALF + GLA_LR

RWKV_HEAD = 64
RWKV_HEADS = MIX_HALF // RWKV_HEAD
RWKV_W_LR = 64
RWKV_A_LR = 64
RWKV_G_LR = 128
RWKV_COLS = 3 * MIX_HALF + RWKV_W_LR + RWKV_A_LR + RWKV_G_LR
RWKV_LN_EPS = 64e-5
EVEN_COLS = GLA_COLS + RWKV_COLS

SB_HEAD = 64
SB_HEADS = MIX_HALF // SB_HEAD
SB_BLOCK = 128

DN_HEADS = 4
DN_HEAD = MIX_HALF // DN_HEADS
DN_CONV = 4
DN_CHUNK = 64
DN_CONV_CH = 3 * MIX_HALF
ODD_COLS = 3 * MIX_HALF + DN_CONV_CH + MIX_HALF + 2 * DN_HEADS

MEM_LEN = 256
X_HEADS = 4
X_HEAD = D_MODEL // X_HEADS

FF = ((8 * D_MODEL + 3 * 256 - 1) // (3 * 256)) * 256

kernel_name = 'hybrid_gla_rwkv7_stickbreak_gdn_step'


def _offsets(sizes):
    return [int(s) for s in np.cumsum(sizes)[:-1]]


def rmsnorm(x, g):
    xf = x.astype(jnp.float32)
    return xf * lax.rsqrt(jnp.mean(xf * xf, axis=-1, keepdims=True) + EPS) * g


def l2norm(x):
    return x * lax.rsqrt(jnp.sum(x * x, axis=-1, keepdims=True) + EPS)


def to_chunks(a, c):
    b, t = a.shape[:2]
    n = -(-t // c)
    a = jnp.pad(a, [(0, 0), (0, n * c - t)] + [(0, 0)] * (a.ndim - 2))
    a = a.reshape((b, n, c) + a.shape[2:])
    return jnp.moveaxis(jnp.moveaxis(a, 1, 0), 3, 2)


def from_chunks(o, t):
    n, b, h, c = o.shape[:4]
    o = jnp.swapaxes(jnp.moveaxis(o, 0, 1), 2, 3)
    return o.reshape((b, n * c, h) + o.shape[4:])[:, :t]


def gla_recurrence(q, k, v, log_f, s0):
    t = q.shape[1]
    c = min(GLA_CHUNK, t)
    qc, kc, vc, fc = (to_chunks(a, c) for a in (q, k, v, log_f))
    bc = jnp.cumsum(fc, axis=3)
    mask = jnp.tril(jnp.ones((c, c), bool))

    def step(s, xs):
        qi, ki, vi, bi = xs
        diff = jnp.where(mask[:, :, None], bi[..., :, None, :] - bi[..., None, :, :], -jnp.inf)
        p = jnp.einsum('bhtd,bhsd,bhtsd->bhts', qi, ki, jnp.exp(diff))
        o = jnp.einsum('bhtd,bhde->bhte', qi * jnp.exp(bi), s) + jnp.einsum('bhts,bhse->bhte', p, vi)
        bl = bi[..., -1:, :]
        s = jnp.exp(bl[..., 0, :])[..., None] * s + jnp.einsum('bhsd,bhse->bhde', ki * jnp.exp(bl - bi), vi)
        return s, o

    s, o = lax.scan(step, s0, (qc, kc, vc, bc))
    return from_chunks(o, t), s


def rwkv_recurrence(r, w, k, v, kk, a, s0):
    def step(s, xs):
        rt, wt, kt, vt, kkt, at = xs
        sa = jnp.einsum('bhij,bhj->bhi', s, -kkt)
        s = s * wt[:, :, None, :] + sa[..., None] * (kkt * at)[:, :, None, :] + vt[..., None] * kt[:, :, None, :]
        return s, jnp.einsum('bhij,bhj->bhi', s, rt)

    xs = tuple(jnp.moveaxis(z, 1, 0) for z in (r, w, k, v, kk, a))
    s, y = lax.scan(step, s0, xs)
    return jnp.moveaxis(y, 0, 1), s


def delta_rule(q, k, v, g, beta, s0):
    t = q.shape[1]
    c = min(DN_CHUNK, t)
    qc, kc, vc = (to_chunks(a, c) for a in (q, k, v))
    gc, bc = (to_chunks(a, c) for a in (g, beta))
    G = jnp.cumsum(gc, axis=-1)
    incl = jnp.tril(jnp.ones((c, c), bool))
    strict = jnp.tril(jnp.ones((c, c), bool), -1)
    dmat = jnp.exp(jnp.where(incl, G[..., :, None] - G[..., None, :], -jnp.inf))
    kk = jnp.einsum('nbhtd,nbhsd->nbhts', kc, kc)
    a_mat = jnp.where(strict, bc[..., :, None] * kk * dmat, 0.0)
    ia = a_mat + jnp.eye(c, dtype=a_mat.dtype)
    tv = lax.linalg.triangular_solve(ia, bc[..., None] * vc, left_side=True, lower=True, unit_diagonal=True)
    tk = lax.linalg.triangular_solve(ia, (bc * jnp.exp(G))[..., None] * kc, left_side=True, lower=True, unit_diagonal=True)
    qk = jnp.where(incl, jnp.einsum('nbhtd,nbhsd->nbhts', qc, kc) * dmat, 0.0)

    def step(s, xs):
        tvi, tki, qi, ki, qki, gi = xs
        u = tvi - jnp.einsum('bhtd,bhde->bhte', tki, s)
        o = jnp.einsum('bhtd,bhde->bhte', qi * jnp.exp(gi)[..., None], s) + jnp.einsum('bhts,bhse->bhte', qki, u)
        gl = gi[..., -1:]
        s = jnp.exp(gl)[..., None] * s + jnp.einsum('bhtd,bhte->bhde', ki * jnp.exp(gl - gi)[..., None], u)
        return s, o

    s, o = lax.scan(step, s0, (tv, tk, qc, kc, qk, G))
    return from_chunks(o, t), s


def causal_conv(x, buf, w):
    t = x.shape[1]
    xp = jnp.concatenate([buf.astype(jnp.float32), x.astype(jnp.float32)], axis=1)
    y = sum(xp[:, i:i + t] * w[i] for i in range(DN_CONV))
    return jax.nn.silu(y), xp[:, t:]


def stick_breaking(q, k, v, bias, q_offset):
    b, t, h, dh = q.shape
    tk = k.shape[1]
    qb = SB_BLOCK if t % SB_BLOCK == 0 else t
    nb = t // qb
    q_blocks = jnp.moveaxis(q.reshape(b, nb, qb, h, dh), 1, 0)
    starts = q_offset + jnp.arange(nb, dtype=jnp.int32) * qb
    kpos = jnp.arange(tk, dtype=jnp.int32)
    kf = k.astype(jnp.float32)
    vf = v.astype(jnp.float32)
    scale = dh ** -0.5

    def block(args):
        qblk, start = args
        z = jnp.einsum('bqhd,bkhd->bhqk', qblk.astype(jnp.float32), kf) * scale + bias[None, :, None, None]
        qpos = start + jnp.arange(qb, dtype=jnp.int32)
        causal = kpos[None, :] < qpos[:, None]
        log_stay = jnp.where(causal, jax.nn.log_sigmoid(-z), 0.0)
        later = lax.cumsum(log_stay, axis=3, reverse=True) - log_stay
        wts = jnp.where(causal, jnp.exp(jax.nn.log_sigmoid(z) + later), 0.0)
        return jnp.einsum('bhqk,bkhd->bqhd', wts, vf)

    o = lax.map(block, (q_blocks, starts))
    return jnp.moveaxis(o, 0, 1).reshape(b, t, h, dh)


def even_mixer(h, p, s_gla, s_rwkv, s_shift):
    f32 = jnp.float32
    b, t, _ = h.shape
    cols = h @ p['w_in']
    g_cols, r_cols = cols[..., :GLA_COLS], cols[..., GLA_COLS:]
    nk = GLA_HEADS * GLA_DK
    gq, gk, gv, gg, glr = jnp.split(g_cols, _offsets([nk, nk, MIX_HALF, MIX_HALF, GLA_LR]), axis=-1)
    q = (gq.reshape(b, t, GLA_HEADS, GLA_DK) * GLA_DK ** -0.5).astype(f32)
    k = gk.reshape(b, t, GLA_HEADS, GLA_DK).astype(f32)
    v = gv.reshape(b, t, GLA_HEADS, GLA_DV).astype(f32)
    log_f = jax.nn.log_sigmoid(glr @ p['gla_w_a2'] + p['gla_b_a']) / GLA_GATE_NORM
    log_f = log_f.reshape(b, t, GLA_HEADS, GLA_DK).astype(f32)
    o_gla, s_gla = gla_recurrence(q, k, v, log_f, s_gla.astype(f32))
    o_gla = rmsnorm(o_gla, p['gla_norm']) * jax.nn.silu(gg.reshape(b, t, GLA_HEADS, GLA_DV))
    o_gla = o_gla.reshape(b, t, MIX_HALF)
    prev = jnp.concatenate([s_shift[:, None].astype(r_cols.dtype), r_cols[:, :-1]], axis=1)
    xr = r_cols + (prev - r_cols) * p['rwkv_mu']
    new_shift = r_cols[:, -1]
    rr, rk, rv, rw, ra, rg = jnp.split(xr, _offsets([MIX_HALF] * 3 + [RWKV_W_LR, RWKV_A_LR, RWKV_G_LR]), axis=-1)
    hs = (b, t, RWKV_HEADS, RWKV_HEAD)
    w_log = -jax.nn.softplus(-(p['rwkv_w0'] + jnp.tanh(rw) @ p['rwkv_w2'])) - 0.5
    decay = jnp.exp(-jnp.exp(w_log))
    a = jax.nn.sigmoid(p['rwkv_a0'] + ra @ p['rwkv_a2'])
    g = jax.nn.sigmoid(rg) @ p['rwkv_g2']
    kk = l2norm((rk * p['rwkv_k_k']).reshape(hs)).astype(f32)
    k2 = rk * (1.0 + (a - 1.0) * p['rwkv_k_a'])
    r4 = rr.reshape(hs).astype(f32)
    k4 = k2.reshape(hs).astype(f32)
    v4 = rv.reshape(hs).astype(f32)
    y, s_rwkv = rwkv_recurrence(r4, decay.reshape(hs).astype(f32), k4, v4, kk,
                                a.reshape(hs).astype(f32), s_rwkv.astype(f32))
    mu = jnp.mean(y, axis=-1, keepdims=True)
    var = jnp.mean(jnp.square(y - mu), axis=-1, keepdims=True)
    y = ((y - mu) * lax.rsqrt(var + RWKV_LN_EPS)).reshape(b, t, MIX_HALF) * p['rwkv_ln_w'] + p['rwkv_ln_b']
    bonus = jnp.sum(r4 * k4 * p['rwkv_r_k'].reshape(RWKV_HEADS, RWKV_HEAD), axis=-1, keepdims=True) * v4
    y = (y + bonus.reshape(b, t, MIX_HALF)) * g
    out = jnp.concatenate([o_gla, y], axis=-1) @ p['w_out']
    return out, s_gla, s_rwkv, new_shift


def odd_mixer(h, p, s_dn, s_conv, k_past, v_past):
    f32 = jnp.float32
    b, t, _ = h.shape
    cols = h @ p['w_in']
    sq, sk, sv, dqkv, dz, da, db = jnp.split(
        cols, _offsets([MIX_HALF] * 3 + [DN_CONV_CH, MIX_HALF, DN_HEADS, DN_HEADS]), axis=-1)
    shs = (b, t, SB_HEADS, SB_HEAD)
    q = rmsnorm(sq.reshape(shs), p['sb_q_norm'])
    k = rmsnorm(sk.reshape(shs), p['sb_k_norm'])
    v = sv.reshape(shs).astype(f32)
    if k_past is None:
        k_all, v_all, offset = k, v, 0
    else:
        k_all = jnp.concatenate([k_past.astype(f32), k], axis=1)
        v_all = jnp.concatenate([v_past.astype(f32), v], axis=1)
        offset = k_past.shape[1]
    o_sb = stick_breaking(q, k_all, v_all, p['sb_bias'].astype(f32), offset).reshape(b, t, MIX_HALF)
    conv_out, s_conv = causal_conv(dqkv, s_conv, p['dn_conv'])
    dq, dk, dv = jnp.split(conv_out, 3, axis=-1)
    dhs = (b, t, DN_HEADS, DN_HEAD)
    dq = l2norm(dq.reshape(dhs)) * DN_HEAD ** -0.5
    dk = l2norm(dk.reshape(dhs))
    dv = dv.reshape(dhs)
    gdec = (-jnp.exp(p['dn_a_log']) * jax.nn.softplus(da + p['dn_dt_bias'])).astype(f32)
    beta = jax.nn.sigmoid(db).astype(f32)
    o_dn, s_dn = delta_rule(dq, dk, dv, gdec, beta, s_dn.astype(f32))
    o_dn = (rmsnorm(o_dn, p['dn_norm']) * jax.nn.silu(dz.reshape(dhs))).reshape(b, t, MIX_HALF)
    out = jnp.concatenate([o_sb, o_dn], axis=-1) @ p['w_out']
    return out, k, v, s_dn, s_conv


def memory_kv(mem, mem_norm, w_xkv, xk_norm):
    b, l, _ = mem.shape
    kv = rmsnorm(mem, mem_norm) @ w_xkv
    k = rmsnorm(kv[..., :D_MODEL].reshape(b, l, X_HEADS, X_HEAD), xk_norm)
    v = kv[..., D_MODEL:].reshape(b, l, X_HEADS, X_HEAD)
    return k, v


def cross_attn(h, mk, mv, w_xq, xq_norm, w_xo):
    b, t, _ = h.shape
    q = rmsnorm((h @ w_xq).reshape(b, t, X_HEADS, X_HEAD), xq_norm)
    s = jnp.einsum('bthd,bmhd->bhtm', q, mk.astype(jnp.float32)) * X_HEAD ** -0.5
    a = jax.nn.softmax(s, axis=-1)
    o = jnp.einsum('bhtm,bmhd->bthd', a, mv.astype(jnp.float32)).reshape(b, t, D_MODEL)
    return o @ w_xo


def swiglu(h, w_gu, w_down):
    gu = h @ w_gu
    return (jax.nn.silu(gu[..., :FF]) * gu[..., FF:]) @ w_down


def setup_inputs(seed: int = 0) -> dict:
    key = jax.random.key(seed)
    keys = iter(jax.random.split(key, 64))

    def nrm(shape, scale=1.0):
        return jax.random.normal(next(keys), shape, jnp.float32) * scale

    def gain(shape):
        return 1.0 + nrm(shape, 0.05)

    def unif(shape, lo, hi):
        return jax.random.uniform(next(keys), shape, jnp.float32, lo, hi)

    n_pages = PAST_LEN // PAGE_SIZE
    n_used = DEC_BATCH * n_pages
    n_phys = n_used + (n_used + 3) // 4
    page_table = jax.random.permutation(next(keys), n_phys)[:n_used].reshape(DEC_BATCH, n_pages).astype(jnp.int32)
    dt = jnp.exp(unif((N_ODD, DN_HEADS), math.log(1e-3), math.log(1e-1)))
    dn_dt_bias = jnp.log(jnp.expm1(dt))
    ds = D_MODEL ** -0.5
    return {
        'x_prompt': nrm((BATCH, SEQ, D_MODEL)),
        'x_sample': nrm((DEC_BATCH, DEC_SEQ, D_MODEL)),
        'mem_prompt': nrm((BATCH, MEM_LEN, D_MODEL)),
        'state_gla': nrm((N_EVEN, DEC_BATCH, GLA_HEADS, GLA_DK, GLA_DV), 0.3),
        'state_rwkv': nrm((N_EVEN, DEC_BATCH, RWKV_HEADS, RWKV_HEAD, RWKV_HEAD), 0.3),
        'state_rwkv_shift': nrm((N_EVEN, DEC_BATCH, RWKV_COLS)),
        'cache_sb_k': nrm((N_ODD, n_phys, PAGE_SIZE, SB_HEADS, SB_HEAD)),
        'cache_sb_v': nrm((N_ODD, n_phys, PAGE_SIZE, SB_HEADS, SB_HEAD)),
        'page_table': page_table,
        'state_delta': nrm((N_ODD, DEC_BATCH, DN_HEADS, DN_HEAD, DN_HEAD), 0.3),
        'state_delta_conv': nrm((N_ODD, DEC_BATCH, DN_CONV - 1, DN_CONV_CH)),
        'cache_mem_k': nrm((DEPTH, DEC_BATCH, MEM_LEN, X_HEADS, X_HEAD)),
        'cache_mem_v': nrm((DEPTH, DEC_BATCH, MEM_LEN, X_HEADS, X_HEAD)),
        'norm_mix': gain((DEPTH, D_MODEL)),
        'norm_cross': gain((DEPTH, D_MODEL)),
        'norm_ffn': gain((DEPTH, D_MODEL)),
        'w_in_even': nrm((N_EVEN, D_MODEL, EVEN_COLS), ds),
        'w_out_even': nrm((N_EVEN, 2 * MIX_HALF, D_MODEL), (2 * MIX_HALF) ** -0.5),
        'gla_w_a2': nrm((N_EVEN, GLA_LR, GLA_HEADS * GLA_DK), GLA_LR ** -0.5),
        'gla_b_a': nrm((N_EVEN, GLA_HEADS * GLA_DK), 0.1),
        'gla_norm': gain((N_EVEN, GLA_DV)),
        'rwkv_mu': unif((N_EVEN, RWKV_COLS), 0.0, 1.0),
        'rwkv_w0': unif((N_EVEN, MIX_HALF), -6.0, -0.5),
        'rwkv_w2': nrm((N_EVEN, RWKV_W_LR, MIX_HALF), 0.1 * RWKV_W_LR ** -0.5),
        'rwkv_a0': nrm((N_EVEN, MIX_HALF), 0.5),
        'rwkv_a2': nrm((N_EVEN, RWKV_A_LR, MIX_HALF), RWKV_A_LR ** -0.5),
        'rwkv_g2': nrm((N_EVEN, RWKV_G_LR, MIX_HALF), RWKV_G_LR ** -0.5),
        'rwkv_k_k': 0.85 + nrm((N_EVEN, MIX_HALF), 0.05),
        'rwkv_k_a': gain((N_EVEN, MIX_HALF)),
        'rwkv_r_k': nrm((N_EVEN, MIX_HALF), 0.1),
        'rwkv_ln_w': gain((N_EVEN, MIX_HALF)),
        'rwkv_ln_b': nrm((N_EVEN, MIX_HALF), 0.05),
        'w_in_odd': nrm((N_ODD, D_MODEL, ODD_COLS), ds),
        'w_out_odd': nrm((N_ODD, 2 * MIX_HALF, D_MODEL), (2 * MIX_HALF) ** -0.5),
        'sb_q_norm': gain((N_ODD, SB_HEAD)),
        'sb_k_norm': gain((N_ODD, SB_HEAD)),
        'sb_bias': unif((N_ODD, SB_HEADS), -7.0, -5.0),
        'dn_conv': nrm((N_ODD, DN_CONV, DN_CONV_CH), DN_CONV ** -0.5),
        'dn_a_log': jnp.log(unif((N_ODD, DN_HEADS), 1.0, 16.0)),
        'dn_dt_bias': dn_dt_bias,
        'dn_norm': gain((N_ODD, DN_HEAD)),
        'mem_norm': gain((DEPTH, D_MODEL)),
        'w_xq': nrm((DEPTH, D_MODEL, D_MODEL), ds),
        'w_xkv': nrm((DEPTH, D_MODEL, 2 * D_MODEL), ds),
        'w_xo': nrm((DEPTH, D_MODEL, D_MODEL), ds),
        'xq_norm': gain((DEPTH, X_HEAD)),
        'xk_norm': gain((DEPTH, X_HEAD)),
        'w_gu': nrm((DEPTH, D_MODEL, 2 * FF), ds),
        'w_down': nrm((DEPTH, FF, D_MODEL), FF ** -0.5),
    }


def reference(x_prompt, x_sample, mem_prompt,
              state_gla, state_rwkv, state_rwkv_shift,
              cache_sb_k, cache_sb_v, page_table, state_delta, state_delta_conv,
              cache_mem_k, cache_mem_v,
              norm_mix, norm_cross, norm_ffn,
              w_in_even, w_out_even, gla_w_a2, gla_b_a, gla_norm,
              rwkv_mu, rwkv_w0, rwkv_w2, rwkv_a0, rwkv_a2, rwkv_g2, rwkv_k_k, rwkv_k_a, rwkv_r_k,
              rwkv_ln_w, rwkv_ln_b,
              w_in_odd, w_out_odd, sb_q_norm, sb_k_norm, sb_bias, dn_conv, dn_a_log, dn_dt_bias, dn_norm,
              mem_norm, w_xq, w_xkv, w_xo, xq_norm, xk_norm,
              w_gu, w_down):
    f32 = jnp.float32
    b = x_prompt.shape[0]
    db = x_sample.shape[0]
    n_past = page_table.shape[1] * PAGE_SIZE
    xp, xs = x_prompt, x_sample
    p_gla, p_rwkv, p_shift, p_sbk, p_sbv, p_dn, p_conv, p_mk, p_mv = [], [], [], [], [], [], [], [], []
    s_gla, s_rwkv, s_shift, s_sbk, s_sbv, s_dn, s_conv = [], [], [], [], [], [], []
    for i in range(DEPTH):
        j = i // 2
        if i % 2 == 0:
            ep = {'w_in': w_in_even[j], 'w_out': w_out_even[j], 'gla_w_a2': gla_w_a2[j],
                  'gla_b_a': gla_b_a[j], 'gla_norm': gla_norm[j], 'rwkv_mu': rwkv_mu[j],
                  'rwkv_w0': rwkv_w0[j], 'rwkv_w2': rwkv_w2[j], 'rwkv_a0': rwkv_a0[j],
                  'rwkv_a2': rwkv_a2[j], 'rwkv_g2': rwkv_g2[j], 'rwkv_k_k': rwkv_k_k[j],
                  'rwkv_k_a': rwkv_k_a[j], 'rwkv_r_k': rwkv_r_k[j], 'rwkv_ln_w': rwkv_ln_w[j],
                  'rwkv_ln_b': rwkv_ln_b[j]}
            o, sg, sr, sh = even_mixer(rmsnorm(xp, norm_mix[i]), ep,
                                       jnp.zeros((b, GLA_HEADS, GLA_DK, GLA_DV), f32),
                                       jnp.zeros((b, RWKV_HEADS, RWKV_HEAD, RWKV_HEAD), f32),
                                       jnp.zeros((b, RWKV_COLS), f32))
            xp = xp + o
            p_gla.append(sg)
            p_rwkv.append(sr)
            p_shift.append(sh)
            o, sg, sr, sh = even_mixer(rmsnorm(xs, norm_mix[i]), ep,
                                       state_gla[j], state_rwkv[j], state_rwkv_shift[j])
            xs = xs + o
            s_gla.append(sg)
            s_rwkv.append(sr)
            s_shift.append(sh)
        else:
            op = {'w_in': w_in_odd[j], 'w_out': w_out_odd[j], 'sb_q_norm': sb_q_norm[j],
                  'sb_k_norm': sb_k_norm[j], 'sb_bias': sb_bias[j], 'dn_conv': dn_conv[j],
                  'dn_a_log': dn_a_log[j], 'dn_dt_bias': dn_dt_bias[j], 'dn_norm': dn_norm[j]}
            o, kn, vn, sd, sc = odd_mixer(rmsnorm(xp, norm_mix[i]), op,
                                          jnp.zeros((b, DN_HEADS, DN_HEAD, DN_HEAD), f32),
                                          jnp.zeros((b, DN_CONV - 1, DN_CONV_CH), f32), None, None)
            xp = xp + o
            p_sbk.append(kn)
            p_sbv.append(vn)
            p_dn.append(sd)
            p_conv.append(sc)
            k_past = cache_sb_k[j][page_table].reshape(db, n_past, SB_HEADS, SB_HEAD)
            v_past = cache_sb_v[j][page_table].reshape(db, n_past, SB_HEADS, SB_HEAD)
            o, kn, vn, sd, sc = odd_mixer(rmsnorm(xs, norm_mix[i]), op,
                                          state_delta[j], state_delta_conv[j], k_past, v_past)
            xs = xs + o
            s_sbk.append(kn)
            s_sbv.append(vn)
            s_dn.append(sd)
            s_conv.append(sc)
        mk, mv = memory_kv(mem_prompt, mem_norm[i], w_xkv[i], xk_norm[i])
        p_mk.append(mk)
        p_mv.append(mv)
        xp = xp + cross_attn(rmsnorm(xp, norm_cross[i]), mk, mv, w_xq[i], xq_norm[i], w_xo[i])
        xs = xs + cross_attn(rmsnorm(xs, norm_cross[i]), cache_mem_k[i], cache_mem_v[i],
                             w_xq[i], xq_norm[i], w_xo[i])
        xp = xp + swiglu(rmsnorm(xp, norm_ffn[i]), w_gu[i], w_down[i])
        xs = xs + swiglu(rmsnorm(xs, norm_ffn[i]), w_gu[i], w_down[i])
    return (xp, xs,
            jnp.stack(p_gla), jnp.stack(p_rwkv), jnp.stack(p_shift),
            jnp.stack(p_sbk), jnp.stack(p_sbv), jnp.stack(p_dn), jnp.stack(p_conv),
            jnp.stack(p_mk), jnp.stack(p_mv),
            jnp.stack(s_gla), jnp.stack(s_rwkv), jnp.stack(s_shift),
            jnp.stack(s_sbk), jnp.stack(s_sbv), jnp.stack(s_dn), jnp.stack(s_conv))
```

```python
import functools
import math

import jax
import jax.numpy as jnp
from jax import lax
from jax.experimental import pallas as pl
from jax.experimental.pallas import tpu as pltpu

F32 = jnp.float32
BF16 = jnp.bfloat16

LANES = 128
SUBLANES = 8
VMEM_LIMIT_BYTES = 56 * 1024 * 1024

EPS = 1e-6
PAGE_SIZE = 128

GLA_HEADS, GLA_DK, GLA_DV, GLA_LR = 4, 64, 128, 16
GLA_GATE_NORM = 16.0
GLA_CHUNK = 64
GLA_SUB = 16
RWKV_HEADS, RWKV_HEAD = 8, 64
RWKV_W_LR, RWKV_A_LR, RWKV_G_LR = 64, 64, 128
RWKV_LN_EPS = 64e-5
RWKV_CHUNK = 64
SB_HEADS, SB_HEAD = 8, 64
DN_HEADS, DN_HEAD, DN_CONV = 4, 128, 4
DN_CHUNK = 64
X_HEADS = 4


def _cparams(*sem):
    return pltpu.CompilerParams(dimension_semantics=sem, vmem_limit_bytes=VMEM_LIMIT_BYTES)


def _round_up(n, m):
    return -(-n // m) * m


def _pick_tile(n, candidates):
    for c in candidates:
        if n % c == 0:
            return c
    return n


def _dot(a, b):
    return jnp.dot(a.astype(BF16), b.astype(BF16), preferred_element_type=F32)


def _dot_nt(a, b):
    return lax.dot_general(a.astype(BF16), b.astype(BF16), (((1,), (1,)), ((), ())),
                           preferred_element_type=F32)


def _dot_tn(a, b):
    return lax.dot_general(a.astype(BF16), b.astype(BF16), (((0,), (0,)), ((), ())),
                           preferred_element_type=F32)


def _split3(a):
    hi = a.astype(BF16)
    r1 = a - hi.astype(F32)
    mid = r1.astype(BF16)
    lo = (r1 - mid.astype(F32)).astype(BF16)
    return hi, mid, lo


def _dot_x3(a, b):
    b = b.astype(BF16)
    hi, mid, lo = _split3(a)
    return (jnp.dot(hi, b, preferred_element_type=F32) + jnp.dot(mid, b, preferred_element_type=F32)
            + jnp.dot(lo, b, preferred_element_type=F32))


def _dot_3x(a, b):
    a = a.astype(BF16)
    hi, mid, lo = _split3(b)
    return (jnp.dot(a, hi, preferred_element_type=F32) + jnp.dot(a, mid, preferred_element_type=F32)
            + jnp.dot(a, lo, preferred_element_type=F32))


def _dot_hp(a, b):
    return jnp.dot(a, b, preferred_element_type=F32, precision=lax.Precision.HIGHEST)


def _iota(shape, dim):
    return lax.broadcasted_iota(jnp.int32, shape, dim)


def _group_ones(n, group):
    return (_iota((n, n), 0) // group == _iota((n, n), 1) // group).astype(F32)


def _group_sum(x, group):
    return _dot_x3(x, _group_ones(x.shape[-1], group))


def _rms(x, gain):
    return x * lax.rsqrt(jnp.mean(x * x, axis=-1, keepdims=True) + EPS) * gain


def _group_rms(x, gain_row, group):
    ms = _group_sum(x * x, group) * (1.0 / group)
    return x * lax.rsqrt(ms + EPS) * gain_row


def _silu(x):
    return x * jax.nn.sigmoid(x)


def _softplus(x):
    return jnp.maximum(x, 0.0) + jnp.log1p(jnp.exp(-jnp.abs(x)))


def _log_sigmoid(x):
    return -_softplus(-x)


def _norm_matmul_kernel(x_ref, g_ref, w_ref, *rest, head_group):
    if head_group:
        hg_ref, o_ref, h_ref = rest
    else:
        o_ref, h_ref = rest

    @pl.when(pl.program_id(1) == 0)
    def _():
        h_ref[...] = _rms(x_ref[...], g_ref[...]).astype(BF16)

    y = jnp.dot(h_ref[...], w_ref[...], preferred_element_type=F32)
    if head_group:
        y = _group_rms(y, hg_ref[...], head_group)
    o_ref[...] = y.astype(o_ref.dtype)


def norm_matmul(x, gain, w, *, head_gain=None, head_group=0, out_dtype=F32):
    m, k = x.shape
    n = w.shape[1]
    tm = _pick_tile(m, (1024, 512, 256, 128))
    tn = _pick_tile(n, (512, 384, 256, 128))
    in_specs = [pl.BlockSpec((tm, k), lambda i, j: (i, 0)),
                pl.BlockSpec((1, k), lambda i, j: (0, 0)),
                pl.BlockSpec((k, tn), lambda i, j: (0, j))]
    args = [x, gain.reshape(1, k), w]
    if head_group:
        in_specs.append(pl.BlockSpec((1, tn), lambda i, j: (0, j)))
        args.append(head_gain.reshape(1, n))
    return pl.pallas_call(
        functools.partial(_norm_matmul_kernel, head_group=head_group),
        grid=(m // tm, n // tn),
        in_specs=in_specs,
        out_specs=pl.BlockSpec((tm, tn), lambda i, j: (i, j)),
        out_shape=jax.ShapeDtypeStruct((m, n), out_dtype),
        scratch_shapes=[pltpu.VMEM((tm, k), BF16)],
        compiler_params=_cparams("parallel", "arbitrary"),
        name="norm_matmul",
    )(*args)


def _matmul_res_kernel(*refs, n_pairs):
    res_ref = refs[2 * n_pairs]
    o_ref = refs[2 * n_pairs + 1]
    acc = res_ref[...]
    for p in range(n_pairs):
        acc = acc + jnp.dot(refs[2 * p][...].astype(BF16), refs[2 * p + 1][...],
                            preferred_element_type=F32)
    o_ref[...] = acc


def matmul_residual(pairs, res):
    m, n = res.shape
    tm = _pick_tile(m, (1024, 512, 256, 128))
    tn = _pick_tile(n, (512, 256, 128))
    in_specs, args = [], []
    for a, w in pairs:
        kk = a.shape[1]
        in_specs.append(pl.BlockSpec((tm, kk), lambda i, j: (i, 0)))
        in_specs.append(pl.BlockSpec((kk, tn), lambda i, j: (0, j)))
        args += [a, w]
    in_specs.append(pl.BlockSpec((tm, tn), lambda i, j: (i, j)))
    args.append(res)
    return pl.pallas_call(
        functools.partial(_matmul_res_kernel, n_pairs=len(pairs)),
        grid=(m // tm, n // tn),
        in_specs=in_specs,
        out_specs=pl.BlockSpec((tm, tn), lambda i, j: (i, j)),
        out_shape=jax.ShapeDtypeStruct((m, n), F32),
        compiler_params=_cparams("parallel", "parallel"),
        name="matmul_residual",
    )(*args)


def _ffn_kernel(x_ref, g_ref, wg_ref, wu_ref, wd_ref, o_ref, h_ref, acc_ref):
    f = pl.program_id(1)

    @pl.when(f == 0)
    def _():
        x = x_ref[...]
        h_ref[...] = _rms(x, g_ref[...]).astype(BF16)
        acc_ref[...] = x

    h = h_ref[...]
    gate = jnp.dot(h, wg_ref[...], preferred_element_type=F32)
    up = jnp.dot(h, wu_ref[...], preferred_element_type=F32)
    act = (_silu(gate) * up).astype(BF16)
    acc_ref[...] += jnp.dot(act, wd_ref[...], preferred_element_type=F32)

    @pl.when(f == pl.num_programs(1) - 1)
    def _():
        o_ref[...] = acc_ref[...]


def ffn(x, gain, w_gu, w_down):
    m, d = x.shape
    ff = w_down.shape[0]
    tm = _pick_tile(m, (1024, 512, 256, 128))
    tf = _pick_tile(ff, (256, 128))
    nf = ff // tf
    return pl.pallas_call(
        _ffn_kernel,
        grid=(m // tm, nf),
        in_specs=[pl.BlockSpec((tm, d), lambda i, f: (i, 0)),
                  pl.BlockSpec((1, d), lambda i, f: (0, 0)),
                  pl.BlockSpec((d, tf), lambda i, f: (0, f)),
                  pl.BlockSpec((d, tf), lambda i, f: (0, f + nf)),
                  pl.BlockSpec((tf, d), lambda i, f: (f, 0))],
        out_specs=pl.BlockSpec((tm, d), lambda i, f: (i, 0)),
        out_shape=jax.ShapeDtypeStruct((m, d), F32),
        scratch_shapes=[pltpu.VMEM((tm, d), BF16), pltpu.VMEM((tm, d), F32)],
        compiler_params=_cparams("parallel", "arbitrary"),
        name="ffn",
    )(x, gain.reshape(1, d), w_gu, w_gu, w_down)


def _softmax_rows(s):
    p = jnp.exp(s - jnp.max(s, axis=-1, keepdims=True))
    return p / jnp.sum(p, axis=-1, keepdims=True)


def _cross_prompt_kernel(x_ref, g_ref, wq_ref, qg_ref, mk_ref, mv_ref, wo_ref, o_ref, *, heads):
    x = x_ref[...]
    d = x.shape[1]
    dh = d // heads
    h = _rms(x, g_ref[...]).astype(BF16)
    q = jnp.dot(h, wq_ref[...], preferred_element_type=F32)
    mk = mk_ref[...].astype(BF16)
    mv = mv_ref[...].astype(BF16)
    outs = []
    for hh in range(heads):
        sl = slice(hh * dh, (hh + 1) * dh)
        qh = _rms(q[:, sl], qg_ref[...])
        s = _dot_nt(qh, mk[:, sl]) * dh ** -0.5
        outs.append(_dot(_softmax_rows(s), mv[:, sl]))
    o = jnp.concatenate(outs, axis=-1)
    o_ref[...] = x + _dot(o, wo_ref[...])


def cross_attn_prompt(x, gain, w_xq, xq_norm, mk, mv, w_xo, *, batch):
    m, d = x.shape
    t = m // batch
    mem_len = mk.shape[0] // batch
    dh = d // X_HEADS
    tm = _pick_tile(t, (512, 256, 128))
    nt = t // tm
    return pl.pallas_call(
        functools.partial(_cross_prompt_kernel, heads=X_HEADS),
        grid=(batch, nt),
        in_specs=[pl.BlockSpec((tm, d), lambda b, i: (b * nt + i, 0)),
                  pl.BlockSpec((1, d), lambda b, i: (0, 0)),
                  pl.BlockSpec((d, d), lambda b, i: (0, 0)),
                  pl.BlockSpec((1, dh), lambda b, i: (0, 0)),
                  pl.BlockSpec((mem_len, d), lambda b, i: (b, 0)),
                  pl.BlockSpec((mem_len, d), lambda b, i: (b, 0)),
                  pl.BlockSpec((d, d), lambda b, i: (0, 0))],
        out_specs=pl.BlockSpec((tm, d), lambda b, i: (b * nt + i, 0)),
        out_shape=jax.ShapeDtypeStruct((m, d), F32),
        compiler_params=_cparams("parallel", "parallel"),
        name="cross_attn_prompt",
    )(x, gain.reshape(1, d), w_xq, xq_norm.reshape(1, dh), mk, mv, w_xo)


def _cross_sample_kernel(q_ref, mk_ref, mv_ref, o_ref, *, heads, layer):
    del layer
    nb = q_ref.shape[0]
    d = q_ref.shape[2]
    dh = d // heads
    rows = _iota((SUBLANES, d), 0)
    head_mask = (rows == _iota((SUBLANES, d), 1) // dh).astype(F32)
    for b in range(nb):
        qh = q_ref[b] * head_mask
        s = _dot_nt(qh, mk_ref[0, b]) * dh ** -0.5
        o8 = _dot(_softmax_rows(s), mv_ref[0, b])
        o_ref[b] = jnp.sum(o8 * head_mask, axis=0, keepdims=True)


def cross_attn_sample(q, cache_k, cache_v, *, layer):
    db, d = q.shape
    mem_len = cache_k.shape[2]
    nb = _pick_tile(db, (4, 2, 1))
    out = pl.pallas_call(
        functools.partial(_cross_sample_kernel, heads=X_HEADS, layer=layer),
        grid=(db // nb,),
        in_specs=[pl.BlockSpec((nb, 1, d), lambda i: (i, 0, 0)),
                  pl.BlockSpec((1, nb, mem_len, d), lambda i: (layer, i, 0, 0)),
                  pl.BlockSpec((1, nb, mem_len, d), lambda i: (layer, i, 0, 0))],
        out_specs=pl.BlockSpec((nb, 1, d), lambda i: (i, 0, 0)),
        out_shape=jax.ShapeDtypeStruct((db, 1, d), F32),
        compiler_params=_cparams("parallel"),
        name="cross_attn_sample",
    )(q.reshape(db, 1, d), cache_k, cache_v)
    return out.reshape(db, d)


def _row_to_col(row):
    n = row.shape[1]
    eye = (_iota((n, n), 0) == _iota((n, n), 1)).astype(F32)
    return jnp.sum(eye * row, axis=-1, keepdims=True)


def _tri(n, strict=False):
    r, c = _iota((n, n), 0), _iota((n, n), 1)
    return ((r > c) if strict else (r >= c)).astype(F32)


def _gla_chunk(q, k, v, lf, s):
    c, dk = q.shape
    sub = GLA_SUB
    nsub = c // sub
    b = _dot_3x(_tri(c), lf)
    o = _dot(q * jnp.exp(b), s)
    t_idx = _iota((c, c), 0)
    s_idx = _iota((c, c), 1)
    p_rows = [jnp.zeros((sub, c), F32)]
    for j in range(1, nsub):
        ref = b[sub * j - 1:sub * j, :]
        qj = q[sub * j:sub * (j + 1), :] * jnp.exp(b[sub * j:sub * (j + 1), :] - ref)
        kj = k * jnp.exp(jnp.minimum(ref - b, 0.0))
        pj = _dot_nt(qj, kj)
        p_rows.append(jnp.where(_iota((sub, c), 1) < sub * j, pj, 0.0))
    p = jnp.concatenate(p_rows, axis=0)
    for i in range(sub):
        ksel = jnp.concatenate(
            [jnp.broadcast_to(k[sub * j + i:sub * j + i + 1, :], (sub, dk)) for j in range(nsub)], axis=0)
        bsel = jnp.concatenate(
            [jnp.broadcast_to(b[sub * j + i:sub * j + i + 1, :], (sub, dk)) for j in range(nsub)], axis=0)
        col = jnp.sum(q * ksel * jnp.exp(jnp.minimum(b - bsel, 0.0)), axis=-1, keepdims=True)
        hit = (s_idx == (t_idx // sub) * sub + i) & (t_idx % sub >= i)
        p = jnp.where(hit, col, p)
    o = o + _dot(p, v)
    bl = b[c - 1:c, :]
    s_new = _row_to_col(jnp.exp(bl)) * s + _dot_tn(k * jnp.exp(bl - b), v)
    return o, s_new


def _gla_prompt_kernel(g_ref, wa_ref, ba_ref, gn_ref, o_ref, sout_ref, s_ref):
    tb = g_ref.shape[0]
    c = GLA_CHUNK
    nk = GLA_HEADS * GLA_DK
    half = GLA_HEADS * GLA_DV
    lr_off = 2 * nk + 2 * half

    @pl.when(pl.program_id(1) == 0)
    def _():
        s_ref[...] = jnp.zeros_like(s_ref)

    def chunk(ci, carry):
        rows = pl.ds(pl.multiple_of(ci * c, c), c)
        lr = g_ref[rows, lr_off:lr_off + LANES]
        lf = _log_sigmoid(_dot(lr, wa_ref[...]) + ba_ref[...]) * (1.0 / GLA_GATE_NORM)
        for h in range(GLA_HEADS):
            q = g_ref[rows, h * GLA_DK:(h + 1) * GLA_DK] * GLA_DK ** -0.5
            k = g_ref[rows, nk + h * GLA_DK:nk + (h + 1) * GLA_DK]
            v = g_ref[rows, 2 * nk + h * GLA_DV:2 * nk + (h + 1) * GLA_DV]
            gate = g_ref[rows, 2 * nk + half + h * GLA_DV:2 * nk + half + (h + 1) * GLA_DV]
            o, s_new = _gla_chunk(q, k, v, lf[:, h * GLA_DK:(h + 1) * GLA_DK], s_ref[h])
            s_ref[h] = s_new
            o_ref[rows, h * GLA_DV:(h + 1) * GLA_DV] = (_rms(o, gn_ref[...]) * _silu(gate)).astype(o_ref.dtype)
        return carry

    lax.fori_loop(0, tb // c, chunk, 0)
    sout_ref[0] = s_ref[...]


def gla_prompt(g_cols, w_a2, b_a, gnorm, *, batch):
    m, width = g_cols.shape
    t = m // batch
    tb = _pick_tile(t, (256, 128, 64))
    nt = t // tb
    nk = GLA_HEADS * GLA_DK
    half = GLA_HEADS * GLA_DV
    wa = jnp.zeros((LANES, nk), F32).at[:GLA_LR].set(w_a2)
    return pl.pallas_call(
        _gla_prompt_kernel,
        grid=(batch, nt),
        in_specs=[pl.BlockSpec((tb, width), lambda b, i: (b * nt + i, 0)),
                  pl.BlockSpec((LANES, nk), lambda b, i: (0, 0)),
                  pl.BlockSpec((1, nk), lambda b, i: (0, 0)),
                  pl.BlockSpec((1, GLA_DV), lambda b, i: (0, 0))],
        out_specs=[pl.BlockSpec((tb, half), lambda b, i: (b * nt + i, 0)),
                   pl.BlockSpec((1, GLA_HEADS, GLA_DK, GLA_DV), lambda b, i: (b, 0, 0, 0))],
        out_shape=[jax.ShapeDtypeStruct((m, half), BF16),
                   jax.ShapeDtypeStruct((batch, GLA_HEADS, GLA_DK, GLA_DV), F32)],
        scratch_shapes=[pltpu.VMEM((GLA_HEADS, GLA_DK, GLA_DV), F32)],
        compiler_params=_cparams("parallel", "arbitrary"),
        name="gla_prompt",
    )(g_cols, wa, b_a.reshape(1, nk), gnorm.reshape(1, GLA_DV))


RWKV_MIX = RWKV_HEADS * RWKV_HEAD
RWKV_COLS = 3 * RWKV_MIX + RWKV_W_LR + RWKV_A_LR + RWKV_G_LR
RWKV_PARAM_ROWS = 8


def _unit_lower_inverse(n_mat):
    c = n_mat.shape[0]
    eye = (_iota((c, c), 0) == _iota((c, c), 1)).astype(F32)
    p = eye + n_mat
    power = n_mat
    span = 2
    while span < c:
        power = _dot_hp(power, power)
        p = p + _dot_hp(p, power)
        span *= 2
    return p


def _rwkv_prep(r, prev, mu, lora_w, lora_a, g2, tab):
    m = RWKV_MIX
    xr = r + (prev - r) * mu
    rr, rk, rv = xr[:, :m], xr[:, m:2 * m], xr[:, 2 * m:3 * m]
    wa_in = xr[:, 3 * m:3 * m + LANES]
    wa_in = jnp.where(_iota(wa_in.shape, 1) < RWKV_W_LR, jnp.tanh(wa_in), wa_in)
    rg = xr[:, 3 * m + LANES:3 * m + 2 * LANES]
    w0, a0, k_k, k_a = tab[0:1], tab[1:2], tab[2:3], tab[3:4]
    w_log = -_softplus(-(w0 + _dot(wa_in, lora_w))) - 0.5
    log_decay = -jnp.exp(w_log)
    a = jax.nn.sigmoid(a0 + _dot(wa_in, lora_a))
    g = _dot(jax.nn.sigmoid(rg), g2)
    kx = rk * k_k
    kk = kx * lax.rsqrt(_group_sum(kx * kx, RWKV_HEAD) + EPS)
    k2 = rk * (1.0 + (a - 1.0) * k_a)
    return rr, k2, rv, kk, a, log_decay, g


def _rwkv_finish(y, rr, k2, rv, g, tab):
    r_k, ln_w, ln_b = tab[4:5], tab[5:6], tab[6:7]
    inv = 1.0 / RWKV_HEAD
    mean = _group_sum(y, RWKV_HEAD) * inv
    yc = y - mean
    var = _group_sum(yc * yc, RWKV_HEAD) * inv
    yn = yc * lax.rsqrt(var + RWKV_LN_EPS) * ln_w + ln_b
    bonus = _group_sum(rr * k2 * r_k, RWKV_HEAD) * rv
    return (yn + bonus) * g


def _rwkv_chunk_head(at, rt, bt, kt, bh, kh, v, decay_c, s):
    c = at.shape[0]
    mm = _dot_nt(jnp.concatenate([at, rt], axis=0), jnp.concatenate([bt, kt], axis=0))
    strict = _tri(c, strict=True)
    incl = _tri(c)
    a_ab = mm[:c, :c] * strict
    a_ak = mm[:c, c:] * strict
    b_rb = mm[c:, :c] * incl
    b_rk = mm[c:, c:] * incl
    t_inv = _unit_lower_inverse(a_ab)
    u = _dot(t_inv, _dot_nt(at, s) + _dot(a_ak, v))
    y = _dot_nt(rt, s) + _dot(b_rb, u) + _dot(b_rk, v)
    s_new = s * decay_c + _dot_tn(u, bh) + _dot_tn(v, kh)
    return y, s_new


def _rwkv_prompt_kernel(r_ref, mu_ref, lw_ref, la_ref, g2_ref, tab_ref, o_ref, sout_ref, shout_ref,
                        s_ref, shift_ref, y_ref, f_ref):
    tb = r_ref.shape[0]
    c = RWKV_CHUNK
    hd = RWKV_HEAD

    @pl.when(pl.program_id(1) == 0)
    def _():
        s_ref[...] = jnp.zeros_like(s_ref)
        shift_ref[...] = jnp.zeros_like(shift_ref)

    r = r_ref[...]
    prev = jnp.where(_iota(r.shape, 0) == 0, shift_ref[...], pltpu.roll(r, 1, axis=0))
    shift_ref[...] = r[tb - 1:tb, :]
    tab = tab_ref[...]
    rr, k2, rv, kk, a, ld, g = _rwkv_prep(r, prev, mu_ref[...], lw_ref[...], la_ref[...], g2_ref[...], tab)
    f_ref[0] = rr
    f_ref[1] = k2
    f_ref[2] = rv
    f_ref[3] = kk
    f_ref[4] = a
    f_ref[5] = ld

    def chunk(ci, carry):
        rows = pl.ds(pl.multiple_of(ci * c, c), c)
        rr_c, k2_c, rv_c, kk_c, a_c, ld_c = (f_ref[i, rows, :] for i in range(6))
        gcum = _dot_3x(_tri(c), ld_c)
        g_end = gcum[c - 1:c, :]
        e_neg = jnp.exp(-gcum)
        e_end = jnp.exp(g_end - gcum)
        beta = kk_c * a_c
        at = -kk_c * jnp.exp(gcum - ld_c)
        rt = rr_c * jnp.exp(gcum)
        bt = beta * e_neg
        kt = k2_c * e_neg
        bh = beta * e_end
        kh = k2_c * e_end
        decay_c = jnp.exp(g_end)
        for h in range(RWKV_HEADS):
            sl = slice(h * hd, (h + 1) * hd)
            y, s_new = _rwkv_chunk_head(at[:, sl], rt[:, sl], bt[:, sl], kt[:, sl], bh[:, sl], kh[:, sl],
                                        rv_c[:, sl], decay_c[:, sl], s_ref[h])
            s_ref[h] = s_new
            y_ref[rows, sl] = y
        return carry

    lax.fori_loop(0, tb // c, chunk, 0)
    o_ref[...] = _rwkv_finish(y_ref[...], rr, k2, rv, g, tab).astype(o_ref.dtype)
    sout_ref[0] = s_ref[...]
    shout_ref[0] = shift_ref[...]


def _rwkv_params(mu, w0, w2, a0, a2, g2, k_k, k_a, r_k, ln_w, ln_b):
    m = RWKV_MIX
    lora_w = jnp.zeros((LANES, m), F32).at[:RWKV_W_LR].set(w2)
    lora_a = jnp.zeros((LANES, m), F32).at[RWKV_W_LR:RWKV_W_LR + RWKV_A_LR].set(a2)
    tab = jnp.stack([w0, a0, k_k, k_a, r_k, ln_w, ln_b, jnp.zeros_like(w0)])
    return mu.reshape(1, RWKV_COLS), lora_w, lora_a, g2, tab


def rwkv_prompt(r_cols, params, *, batch):
    m, width = r_cols.shape
    t = m // batch
    tb = _pick_tile(t, (256, 128, 64))
    nt = t // tb
    mix = RWKV_MIX
    mu, lora_w, lora_a, g2, tab = params
    const = lambda b, i: (0, 0)
    return pl.pallas_call(
        _rwkv_prompt_kernel,
        grid=(batch, nt),
        in_specs=[pl.BlockSpec((tb, width), lambda b, i: (b * nt + i, 0)),
                  pl.BlockSpec((1, width), const),
                  pl.BlockSpec((LANES, mix), const),
                  pl.BlockSpec((LANES, mix), const),
                  pl.BlockSpec((RWKV_G_LR, mix), const),
                  pl.BlockSpec((RWKV_PARAM_ROWS, mix), const)],
        out_specs=[pl.BlockSpec((tb, mix), lambda b, i: (b * nt + i, 0)),
                   pl.BlockSpec((1, RWKV_HEADS, RWKV_HEAD, RWKV_HEAD), lambda b, i: (b, 0, 0, 0)),
                   pl.BlockSpec((1, 1, width), lambda b, i: (b, 0, 0))],
        out_shape=[jax.ShapeDtypeStruct((m, mix), BF16),
                   jax.ShapeDtypeStruct((batch, RWKV_HEADS, RWKV_HEAD, RWKV_HEAD), F32),
                   jax.ShapeDtypeStruct((batch, 1, width), F32)],
        scratch_shapes=[pltpu.VMEM((RWKV_HEADS, RWKV_HEAD, RWKV_HEAD), F32),
                        pltpu.VMEM((1, width), F32),
                        pltpu.VMEM((tb, mix), F32),
                        pltpu.VMEM((6, tb, mix), F32)],
        compiler_params=_cparams("parallel", "arbitrary"),
        name="rwkv_prompt",
    )(r_cols, mu, lora_w, lora_a, g2, tab)


DN_MIX = DN_HEADS * DN_HEAD
DN_CONV_CH = 3 * DN_MIX
DN_GATE_OFF = DN_CONV_CH + DN_MIX


def _col_to_row(col):
    n = col.shape[0]
    eye = (_iota((n, n), 0) == _iota((n, n), 1)).astype(F32)
    return jnp.sum(eye * col, axis=0, keepdims=True)


def _dn_gates(gb_tile, a_log_row, dt_bias_row):
    gdec = -jnp.exp(a_log_row) * _softplus(gb_tile + dt_bias_row)
    beta = jax.nn.sigmoid(gb_tile)
    return gdec, beta


def _l2n(x):
    return x * lax.rsqrt(jnp.sum(x * x, axis=-1, keepdims=True) + EPS)


def _dn_chunk_head(q, k, v, g_col, beta_col, s):
    c = q.shape[0]
    incl = _tri(c)
    strict = _tri(c, strict=True)
    dmat = jnp.exp(jnp.minimum(g_col - _col_to_row(g_col), 0.0)) * incl
    kk = _dot_nt(k, k)
    t_inv = _unit_lower_inverse(-(strict * beta_col * kk * dmat))
    tv = _dot(t_inv, beta_col * v)
    tk = _dot(t_inv, (beta_col * jnp.exp(g_col)) * k)
    qk = _dot_nt(q, k) * dmat
    u = tv - _dot(tk, s)
    o = _dot(q * jnp.exp(g_col), s) + _dot(qk, u)
    g_last = g_col[c - 1:c, :]
    s_new = jnp.exp(g_last) * s + _dot_tn(k * jnp.exp(g_last - g_col), u)
    return o, s_new


def _shift_rows(x, k, before):
    out = pltpu.roll(x, k, axis=0)
    row = _iota(x.shape, 0)
    nb = DN_CONV - 1
    for r in range(k):
        out = jnp.where(row == r, before[nb - k + r:nb - k + r + 1, :], out)
    return out


def _dn_prompt_kernel(c_ref, cw_ref, al_ref, dtb_ref, gn_ref, o_ref, sout_ref, cout_ref,
                      s_ref, buf_ref, x_ref, g_ref):
    tb = c_ref.shape[0]
    c = DN_CHUNK
    hd = DN_HEAD
    nb = DN_CONV - 1

    @pl.when(pl.program_id(1) == 0)
    def _():
        s_ref[...] = jnp.zeros_like(s_ref)
        buf_ref[...] = jnp.zeros_like(buf_ref)

    x = c_ref[:, :DN_CONV_CH]
    before = buf_ref[...]
    cw = cw_ref[...]
    y = x * cw[nb:nb + 1, :]
    for k in range(1, DN_CONV):
        y = y + _shift_rows(x, k, before) * cw[nb - k:nb - k + 1, :]
    buf_ref[...] = x[tb - nb:tb, :]
    x_ref[...] = _silu(y)
    gdec, beta = _dn_gates(c_ref[:, DN_GATE_OFF:DN_GATE_OFF + LANES], al_ref[...], dtb_ref[...])
    g_ref[0] = gdec
    g_ref[1] = beta

    def chunk(ci, carry):
        rows = pl.ds(pl.multiple_of(ci * c, c), c)
        gcum = _dot_3x(_tri(c), g_ref[0, rows, :])
        beta_c = g_ref[1, rows, :]
        for h in range(DN_HEADS):
            q = _l2n(x_ref[rows, h * hd:(h + 1) * hd]) * hd ** -0.5
            k = _l2n(x_ref[rows, DN_MIX + h * hd:DN_MIX + (h + 1) * hd])
            v = x_ref[rows, 2 * DN_MIX + h * hd:2 * DN_MIX + (h + 1) * hd]
            o, s_new = _dn_chunk_head(q, k, v, gcum[:, h:h + 1], beta_c[:, DN_HEADS + h:DN_HEADS + h + 1],
                                      s_ref[h])
            s_ref[h] = s_new
            z = c_ref[rows, DN_CONV_CH + h * hd:DN_CONV_CH + (h + 1) * hd]
            o_ref[rows, h * hd:(h + 1) * hd] = (_rms(o, gn_ref[...]) * _silu(z)).astype(o_ref.dtype)
        return carry

    lax.fori_loop(0, tb // c, chunk, 0)
    sout_ref[0] = s_ref[...]
    cout_ref[0] = buf_ref[...]


def _dn_params(conv_w, a_log, dt_bias, gnorm):
    al = jnp.zeros((1, LANES), F32).at[0, :DN_HEADS].set(a_log)
    dtb = jnp.zeros((1, LANES), F32).at[0, :DN_HEADS].set(dt_bias)
    return conv_w, al, dtb, gnorm.reshape(1, DN_HEAD)


def dn_prompt(d_cols, params, *, batch):
    m, width = d_cols.shape
    t = m // batch
    tb = _pick_tile(t, (256, 128, 64))
    nt = t // tb
    conv_w, al, dtb, gn = params
    const = lambda b, i: (0, 0)
    nb = DN_CONV - 1
    return pl.pallas_call(
        _dn_prompt_kernel,
        grid=(batch, nt),
        in_specs=[pl.BlockSpec((tb, width), lambda b, i: (b * nt + i, 0)),
                  pl.BlockSpec((DN_CONV, DN_CONV_CH), const),
                  pl.BlockSpec((1, LANES), const),
                  pl.BlockSpec((1, LANES), const),
                  pl.BlockSpec((1, DN_HEAD), const)],
        out_specs=[pl.BlockSpec((tb, DN_MIX), lambda b, i: (b * nt + i, 0)),
                   pl.BlockSpec((1, DN_HEADS, DN_HEAD, DN_HEAD), lambda b, i: (b, 0, 0, 0)),
                   pl.BlockSpec((1, nb, DN_CONV_CH), lambda b, i: (b, 0, 0))],
        out_shape=[jax.ShapeDtypeStruct((m, DN_MIX), BF16),
                   jax.ShapeDtypeStruct((batch, DN_HEADS, DN_HEAD, DN_HEAD), F32),
                   jax.ShapeDtypeStruct((batch, nb, DN_CONV_CH), F32)],
        scratch_shapes=[pltpu.VMEM((DN_HEADS, DN_HEAD, DN_HEAD), F32),
                        pltpu.VMEM((nb, DN_CONV_CH), F32),
                        pltpu.VMEM((tb, DN_CONV_CH), F32),
                        pltpu.VMEM((2, tb, LANES), F32)],
        compiler_params=_cparams("parallel", "arbitrary"),
        name="dn_prompt",
    )(d_cols, conv_w, al, dtb, gn)


SB_MIX = SB_HEADS * SB_HEAD
SB_PAIR = LANES // SB_HEAD


def _sb_block(z, causal, carry, v):
    tk = z.shape[1]
    log_stay = -_softplus(z)
    if causal is not None:
        log_stay = jnp.where(causal, log_stay, 0.0)
    upper = (_iota((tk, tk), 0) > _iota((tk, tk), 1)).astype(F32)
    later = _dot_x3(log_stay, upper) + carry
    w = jnp.exp(z + log_stay + later)
    if causal is not None:
        w = jnp.where(causal, w, 0.0)
    return _dot(w, v), carry + jnp.sum(log_stay, axis=-1, keepdims=True)


def _sb_prompt_kernel(q_ref, k_ref, v_ref, b_ref, o_ref):
    tq = q_ref.shape[0]
    i = pl.program_id(2)
    scale = SB_HEAD ** -0.5
    q_pos = i * tq + _iota((tq, tq), 0)

    def body(step, carry):
        j = i - step
        rows = pl.ds(pl.multiple_of(j * tq, tq), tq)
        causal = (j * tq + _iota((tq, tq), 1)) < q_pos
        out = []
        for r in range(SB_PAIR):
            sl = slice(r * SB_HEAD, (r + 1) * SB_HEAD)
            acc, run = carry[r]
            z = _dot_nt(q_ref[:, sl], k_ref[rows, sl]) * scale + b_ref[0, r:r + 1, 0:1]
            pv, run = _sb_block(z, causal, run, v_ref[rows, sl])
            out.append((acc + pv, run))
        return tuple(out)

    init = tuple((jnp.zeros((tq, SB_HEAD), F32), jnp.zeros((tq, 1), F32)) for _ in range(SB_PAIR))
    res = lax.fori_loop(0, i + 1, body, init)
    o_ref[...] = jnp.concatenate([res[r][0] for r in range(SB_PAIR)], axis=-1).astype(o_ref.dtype)


def sb_prompt(q, k, v, bias, *, batch):
    m, width = q.shape
    t = m // batch
    tq = _pick_tile(t, (256, 128))
    nq = t // tq
    npair = SB_HEADS // SB_PAIR
    bias_rows = jnp.broadcast_to(bias.reshape(npair, SB_PAIR, 1), (npair, SB_PAIR, LANES))
    return pl.pallas_call(
        _sb_prompt_kernel,
        grid=(batch, npair, nq),
        in_specs=[pl.BlockSpec((tq, LANES), lambda b, p, i: (b * nq + i, p)),
                  pl.BlockSpec((t, LANES), lambda b, p, i: (b, p)),
                  pl.BlockSpec((t, LANES), lambda b, p, i: (b, p)),
                  pl.BlockSpec((1, SB_PAIR, LANES), lambda b, p, i: (p, 0, 0))],
        out_specs=pl.BlockSpec((tq, LANES), lambda b, p, i: (b * nq + i, p)),
        out_shape=jax.ShapeDtypeStruct((m, width), BF16),
        compiler_params=_cparams("parallel", "parallel", "parallel"),
        name="sb_prompt",
    )(q, k, v, bias_rows)


STEP_ROWS = 8


def _gla_step_kernel(g_ref, wa_ref, ba_ref, gn_ref, s_ref, o_ref, sout_ref):
    nk = GLA_HEADS * GLA_DK
    half = GLA_HEADS * GLA_DV
    lr_off = 2 * nk + 2 * half
    lf = _log_sigmoid(_dot(g_ref[:, lr_off:lr_off + LANES], wa_ref[...]) + ba_ref[...]) * (1.0 / GLA_GATE_NORM)
    decay = jnp.exp(lf)
    for b in range(g_ref.shape[0]):
        for h in range(GLA_HEADS):
            q_col = _row_to_col(g_ref[b:b + 1, h * GLA_DK:(h + 1) * GLA_DK] * GLA_DK ** -0.5)
            k_col = _row_to_col(g_ref[b:b + 1, nk + h * GLA_DK:nk + (h + 1) * GLA_DK])
            f_col = _row_to_col(decay[b:b + 1, h * GLA_DK:(h + 1) * GLA_DK])
            v = g_ref[b:b + 1, 2 * nk + h * GLA_DV:2 * nk + (h + 1) * GLA_DV]
            gate = g_ref[b:b + 1, 2 * nk + half + h * GLA_DV:2 * nk + half + (h + 1) * GLA_DV]
            s_new = f_col * s_ref[0, b, h] + k_col * v
            sout_ref[b, h] = s_new
            o = jnp.sum(q_col * s_new, axis=0, keepdims=True)
            o_ref[b:b + 1, h * GLA_DV:(h + 1) * GLA_DV] = _rms(o, gn_ref[...]) * _silu(gate)


def gla_step(g_cols, w_a2, b_a, gnorm, state, *, layer):
    db, width = g_cols.shape
    nk = GLA_HEADS * GLA_DK
    half = GLA_HEADS * GLA_DV
    wa = jnp.zeros((LANES, nk), F32).at[:GLA_LR].set(w_a2)
    nb = STEP_ROWS
    return pl.pallas_call(
        _gla_step_kernel,
        grid=(db // nb,),
        in_specs=[pl.BlockSpec((nb, width), lambda i: (i, 0)),
                  pl.BlockSpec((LANES, nk), lambda i: (0, 0)),
                  pl.BlockSpec((1, nk), lambda i: (0, 0)),
                  pl.BlockSpec((1, GLA_DV), lambda i: (0, 0)),
                  pl.BlockSpec((1, nb, GLA_HEADS, GLA_DK, GLA_DV), lambda i: (layer, i, 0, 0, 0))],
        out_specs=[pl.BlockSpec((nb, half), lambda i: (i, 0)),
                   pl.BlockSpec((nb, GLA_HEADS, GLA_DK, GLA_DV), lambda i: (i, 0, 0, 0))],
        out_shape=[jax.ShapeDtypeStruct((db, half), F32),
                   jax.ShapeDtypeStruct((db, GLA_HEADS, GLA_DK, GLA_DV), F32)],
        compiler_params=_cparams("parallel"),
        name="gla_step",
    )(g_cols, wa, b_a.reshape(1, nk), gnorm.reshape(1, GLA_DV), state)


def _rwkv_step_kernel(r_ref, prev_ref, mu_ref, lw_ref, la_ref, g2_ref, tab_ref, s_ref, o_ref, sout_ref, y_ref):
    hd = RWKV_HEAD
    tab = tab_ref[...]
    rr, k2, rv, kk, a, ld, g = _rwkv_prep(r_ref[...], prev_ref[0], mu_ref[...], lw_ref[...], la_ref[...],
                                          g2_ref[...], tab)
    w = jnp.exp(ld)
    kka = kk * a
    for b in range(r_ref.shape[0]):
        for h in range(RWKV_HEADS):
            sl = slice(h * hd, (h + 1) * hd)
            s = s_ref[0, b, h]
            sa_col = jnp.sum(s * -kk[b:b + 1, sl], axis=1, keepdims=True)
            s_new = s * w[b:b + 1, sl] + sa_col * kka[b:b + 1, sl] + _row_to_col(rv[b:b + 1, sl]) * k2[b:b + 1, sl]
            sout_ref[b, h] = s_new
            y_col = jnp.sum(s_new * rr[b:b + 1, sl], axis=1, keepdims=True)
            y_ref[b:b + 1, sl] = _col_to_row(y_col)
    o_ref[...] = _rwkv_finish(y_ref[...], rr, k2, rv, g, tab)


def rwkv_step(r_cols, params, state, shift, *, layer):
    db, width = r_cols.shape
    mix = RWKV_MIX
    mu, lora_w, lora_a, g2, tab = params
    nb = STEP_ROWS
    const = lambda i: (0, 0)
    return pl.pallas_call(
        _rwkv_step_kernel,
        grid=(db // nb,),
        in_specs=[pl.BlockSpec((nb, width), lambda i: (i, 0)),
                  pl.BlockSpec((1, nb, width), lambda i: (layer, i, 0)),
                  pl.BlockSpec((1, width), const),
                  pl.BlockSpec((LANES, mix), const),
                  pl.BlockSpec((LANES, mix), const),
                  pl.BlockSpec((RWKV_G_LR, mix), const),
                  pl.BlockSpec((RWKV_PARAM_ROWS, mix), const),
                  pl.BlockSpec((1, nb, RWKV_HEADS, RWKV_HEAD, RWKV_HEAD), lambda i: (layer, i, 0, 0, 0))],
        out_specs=[pl.BlockSpec((nb, mix), lambda i: (i, 0)),
                   pl.BlockSpec((nb, RWKV_HEADS, RWKV_HEAD, RWKV_HEAD), lambda i: (i, 0, 0, 0))],
        out_shape=[jax.ShapeDtypeStruct((db, mix), F32),
                   jax.ShapeDtypeStruct((db, RWKV_HEADS, RWKV_HEAD, RWKV_HEAD), F32)],
        scratch_shapes=[pltpu.VMEM((nb, mix), F32)],
        compiler_params=_cparams("parallel"),
        name="rwkv_step",
    )(r_cols, shift, mu, lora_w, lora_a, g2, tab, state)


def _dn_step_kernel(c_ref, cw_ref, al_ref, dtb_ref, gn_ref, s_ref, buf_ref, o_ref, sout_ref, cout_ref):
    hd = DN_HEAD
    nb = DN_CONV - 1
    cw = cw_ref[...]
    gdec, beta = _dn_gates(c_ref[:, DN_GATE_OFF:DN_GATE_OFF + LANES], al_ref[...], dtb_ref[...])
    decay = jnp.exp(gdec)
    for b in range(c_ref.shape[0]):
        x = c_ref[b:b + 1, :DN_CONV_CH]
        before = buf_ref[0, b]
        y = x * cw[nb:nb + 1, :]
        for k in range(nb):
            y = y + before[k:k + 1, :] * cw[k:k + 1, :]
        cout_ref[b] = jnp.concatenate([before[1:nb, :], x], axis=0)
        xc = _silu(y)
        for h in range(DN_HEADS):
            q = _l2n(xc[:, h * hd:(h + 1) * hd]) * hd ** -0.5
            k = _l2n(xc[:, DN_MIX + h * hd:DN_MIX + (h + 1) * hd])
            v = xc[:, 2 * DN_MIX + h * hd:2 * DN_MIX + (h + 1) * hd]
            dec = decay[b:b + 1, h:h + 1]
            bet = beta[b:b + 1, DN_HEADS + h:DN_HEADS + h + 1]
            s = s_ref[0, b, h]
            k_col = _row_to_col(k)
            u = bet * (v - dec * jnp.sum(k_col * s, axis=0, keepdims=True))
            s_new = dec * s + k_col * u
            sout_ref[b, h] = s_new
            o = jnp.sum(_row_to_col(q) * s_new, axis=0, keepdims=True)
            z = c_ref[b:b + 1, DN_CONV_CH + h * hd:DN_CONV_CH + (h + 1) * hd]
            o_ref[b:b + 1, h * hd:(h + 1) * hd] = _rms(o, gn_ref[...]) * _silu(z)


def dn_step(d_cols, params, state, conv_buf, *, layer):
    db, width = d_cols.shape
    conv_w, al, dtb, gn = params
    nb = STEP_ROWS
    nc = DN_CONV - 1
    const = lambda i: (0, 0)
    return pl.pallas_call(
        _dn_step_kernel,
        grid=(db // nb,),
        in_specs=[pl.BlockSpec((nb, width), lambda i: (i, 0)),
                  pl.BlockSpec((DN_CONV, DN_CONV_CH), const),
                  pl.BlockSpec((1, LANES), const),
                  pl.BlockSpec((1, LANES), const),
                  pl.BlockSpec((1, DN_HEAD), const),
                  pl.BlockSpec((1, nb, DN_HEADS, DN_HEAD, DN_HEAD), lambda i: (layer, i, 0, 0, 0)),
                  pl.BlockSpec((1, nb, nc, DN_CONV_CH), lambda i: (layer, i, 0, 0))],
        out_specs=[pl.BlockSpec((nb, DN_MIX), lambda i: (i, 0)),
                   pl.BlockSpec((nb, DN_HEADS, DN_HEAD, DN_HEAD), lambda i: (i, 0, 0, 0)),
                   pl.BlockSpec((nb, nc, DN_CONV_CH), lambda i: (i, 0, 0))],
        out_shape=[jax.ShapeDtypeStruct((db, DN_MIX), F32),
                   jax.ShapeDtypeStruct((db, DN_HEADS, DN_HEAD, DN_HEAD), F32),
                   jax.ShapeDtypeStruct((db, nc, DN_CONV_CH), F32)],
        compiler_params=_cparams("parallel"),
        name="dn_step",
    )(d_cols, conv_w, al, dtb, gn, state, conv_buf)


def _sb_step_kernel(pt_ref, q_ref, b_ref, *refs, n_pages):
    del pt_ref
    k_refs = refs[:n_pages]
    v_refs = refs[n_pages:2 * n_pages]
    o_ref = refs[2 * n_pages]
    width = q_ref.shape[2]
    head_mask = (_iota((SB_HEADS, width), 0) == _iota((SB_HEADS, width), 1) // SB_HEAD).astype(F32)
    qh = q_ref[0] * head_mask
    bias = b_ref[:, 0:1]
    acc = jnp.zeros((SB_HEADS, width), F32)
    run = jnp.zeros((SB_HEADS, 1), F32)
    for p in reversed(range(n_pages)):
        z = _dot_nt(qh, k_refs[p][0, 0]) * SB_HEAD ** -0.5 + bias
        pv, run = _sb_block(z, None, run, v_refs[p][0, 0])
        acc = acc + pv
    o_ref[0] = jnp.sum(acc * head_mask, axis=0, keepdims=True)


def sb_step(q, bias, cache_k, cache_v, page_table, *, layer):
    db, width = q.shape
    n_pages = page_table.shape[1]
    page = cache_k.shape[2]
    bias_rows = jnp.broadcast_to(bias.reshape(SB_HEADS, 1), (SB_HEADS, LANES))

    def page_spec(p):
        return pl.BlockSpec((1, 1, page, width), lambda b, pt: (layer, pt[b, p], 0, 0))

    grid_spec = pltpu.PrefetchScalarGridSpec(
        num_scalar_prefetch=1,
        grid=(db,),
        in_specs=[pl.BlockSpec((1, 1, width), lambda b, pt: (b, 0, 0)),
                  pl.BlockSpec((SB_HEADS, LANES), lambda b, pt: (0, 0))]
                 + [page_spec(p) for p in range(n_pages)] * 2,
        out_specs=pl.BlockSpec((1, 1, width), lambda b, pt: (b, 0, 0)))
    out = pl.pallas_call(
        functools.partial(_sb_step_kernel, n_pages=n_pages),
        grid_spec=grid_spec,
        out_shape=jax.ShapeDtypeStruct((db, 1, width), F32),
        compiler_params=_cparams("parallel"),
        name="sb_step",
    )(page_table, q.reshape(db, 1, width), bias_rows, *([cache_k] * n_pages), *([cache_v] * n_pages))
    return out.reshape(db, width)


PROJ_PAD = 256


def _pad_cols(w, mult):
    n = w.shape[1]
    return jnp.pad(w, ((0, 0), (0, _round_up(n, mult) - n)))


def kernel(x_prompt, x_sample, mem_prompt, state_gla, state_rwkv, state_rwkv_shift, cache_sb_k, cache_sb_v, page_table, state_delta, state_delta_conv, cache_mem_k, cache_mem_v, norm_mix, norm_cross, norm_ffn, w_in_even, w_out_even, gla_w_a2, gla_b_a, gla_norm, rwkv_mu, rwkv_w0, rwkv_w2, rwkv_a0, rwkv_a2, rwkv_g2, rwkv_k_k, rwkv_k_a, rwkv_r_k, rwkv_ln_w, rwkv_ln_b, w_in_odd, w_out_odd, sb_q_norm, sb_k_norm, sb_bias, dn_conv, dn_a_log, dn_dt_bias, dn_norm, mem_norm, w_xq, w_xkv, w_xo, xq_norm, xk_norm, w_gu, w_down):
    b, t, d = x_prompt.shape
    db = x_sample.shape[0]
    depth = norm_mix.shape[0]
    mem_len = mem_prompt.shape[1]
    xh = d // X_HEADS
    gla_cols = 2 * GLA_HEADS * GLA_DK + 2 * GLA_HEADS * GLA_DV + GLA_LR

    xp = x_prompt.reshape(b * t, d)
    xs = x_sample.reshape(db, d)
    mem = mem_prompt.reshape(b * mem_len, d)
    mem_k = cache_mem_k.reshape(depth, db, mem_len, d)
    mem_v = cache_mem_v.reshape(depth, db, mem_len, d)
    sb_k = cache_sb_k.reshape(cache_sb_k.shape[0], cache_sb_k.shape[1], cache_sb_k.shape[2], SB_MIX)
    sb_v = cache_sb_v.reshape(sb_k.shape)

    p_gla, p_rwkv, p_shift, p_sbk, p_sbv, p_dn, p_conv, p_mk, p_mv = ([] for _ in range(9))
    s_gla, s_rwkv, s_shift, s_sbk, s_sbv, s_dn, s_conv = ([] for _ in range(7))

    for i in range(depth):
        j = i // 2
        if i % 2 == 0:
            w_in = w_in_even[j].astype(BF16)
            w_g = _pad_cols(w_in[:, :gla_cols], PROJ_PAD)
            w_r = w_in[:, gla_cols:]
            w_out = w_out_even[j].astype(BF16)
            wo_a, wo_b = w_out[:GLA_HEADS * GLA_DV], w_out[GLA_HEADS * GLA_DV:]
            rp = _rwkv_params(rwkv_mu[j], rwkv_w0[j], rwkv_w2[j], rwkv_a0[j], rwkv_a2[j], rwkv_g2[j],
                              rwkv_k_k[j], rwkv_k_a[j], rwkv_r_k[j], rwkv_ln_w[j], rwkv_ln_b[j])
            g_cols = norm_matmul(xp, norm_mix[i], w_g)
            r_cols = norm_matmul(xp, norm_mix[i], w_r)
            o_a, sg = gla_prompt(g_cols, gla_w_a2[j], gla_b_a[j], gla_norm[j], batch=b)
            o_b, sr, sh = rwkv_prompt(r_cols, rp, batch=b)
            xp = matmul_residual([(o_a, wo_a), (o_b, wo_b)], xp)
            p_gla.append(sg)
            p_rwkv.append(sr)
            p_shift.append(sh.reshape(b, -1))
            g_cols = norm_matmul(xs, norm_mix[i], w_g)
            r_cols = norm_matmul(xs, norm_mix[i], w_r)
            o_a, sg = gla_step(g_cols, gla_w_a2[j], gla_b_a[j], gla_norm[j], state_gla, layer=j)
            o_b, sr = rwkv_step(r_cols, rp, state_rwkv, state_rwkv_shift, layer=j)
            xs = matmul_residual([(o_a, wo_a), (o_b, wo_b)], xs)
            s_gla.append(sg)
            s_rwkv.append(sr)
            s_shift.append(r_cols)
        else:
            w_in = w_in_odd[j].astype(BF16)
            w_q, w_k, w_v = (w_in[:, n * SB_MIX:(n + 1) * SB_MIX] for n in range(3))
            w_d = _pad_cols(w_in[:, 3 * SB_MIX:], PROJ_PAD)
            w_out = w_out_odd[j].astype(BF16)
            wo_a, wo_b = w_out[:SB_MIX], w_out[SB_MIX:]
            qn = jnp.tile(sb_q_norm[j], SB_HEADS)
            kn = jnp.tile(sb_k_norm[j], SB_HEADS)
            dp = _dn_params(dn_conv[j], dn_a_log[j], dn_dt_bias[j], dn_norm[j])
            q = norm_matmul(xp, norm_mix[i], w_q, head_gain=qn, head_group=SB_HEAD)
            k = norm_matmul(xp, norm_mix[i], w_k, head_gain=kn, head_group=SB_HEAD)
            v = norm_matmul(xp, norm_mix[i], w_v)
            d_cols = norm_matmul(xp, norm_mix[i], w_d)
            o_a = sb_prompt(q, k, v, sb_bias[j], batch=b)
            o_b, sd, sc = dn_prompt(d_cols, dp, batch=b)
            xp = matmul_residual([(o_a, wo_a), (o_b, wo_b)], xp)
            p_sbk.append(k.reshape(b, t, SB_HEADS, SB_HEAD))
            p_sbv.append(v.reshape(b, t, SB_HEADS, SB_HEAD))
            p_dn.append(sd)
            p_conv.append(sc)
            q = norm_matmul(xs, norm_mix[i], w_q, head_gain=qn, head_group=SB_HEAD)
            k = norm_matmul(xs, norm_mix[i], w_k, head_gain=kn, head_group=SB_HEAD)
            v = norm_matmul(xs, norm_mix[i], w_v)
            d_cols = norm_matmul(xs, norm_mix[i], w_d)
            o_a = sb_step(q, sb_bias[j], sb_k, sb_v, page_table, layer=j)
            o_b, sd, sc = dn_step(d_cols, dp, state_delta, state_delta_conv, layer=j)
            xs = matmul_residual([(o_a, wo_a), (o_b, wo_b)], xs)
            s_sbk.append(k.reshape(db, 1, SB_HEADS, SB_HEAD))
            s_sbv.append(v.reshape(db, 1, SB_HEADS, SB_HEAD))
            s_dn.append(sd)
            s_conv.append(sc)

        w_kv = w_xkv[i].astype(BF16)
        wq = w_xq[i].astype(BF16)
        wo = w_xo[i].astype(BF16)
        mk = norm_matmul(mem, mem_norm[i], w_kv[:, :d], head_gain=jnp.tile(xk_norm[i], X_HEADS), head_group=xh)
        mv = norm_matmul(mem, mem_norm[i], w_kv[:, d:])
        p_mk.append(mk.reshape(b, mem_len, X_HEADS, xh))
        p_mv.append(mv.reshape(b, mem_len, X_HEADS, xh))
        xp = cross_attn_prompt(xp, norm_cross[i], wq, xq_norm[i], mk, mv, wo, batch=b)
        q = norm_matmul(xs, norm_cross[i], wq, head_gain=jnp.tile(xq_norm[i], X_HEADS), head_group=xh)
        xs = matmul_residual([(cross_attn_sample(q, mem_k, mem_v, layer=i), wo)], xs)

        wgu = w_gu[i].astype(BF16)
        wdn = w_down[i].astype(BF16)
        xp = ffn(xp, norm_ffn[i], wgu, wdn)
        xs = ffn(xs, norm_ffn[i], wgu, wdn)

    return (xp.reshape(b, t, d), xs.reshape(db, 1, d),
            jnp.stack(p_gla), jnp.stack(p_rwkv), jnp.stack(p_shift),
            jnp.stack(p_sbk), jnp.stack(p_sbv), jnp.stack(p_dn), jnp.stack(p_conv),
            jnp.stack(p_mk), jnp.stack(p_mv),
            jnp.stack(s_gla), jnp.stack(s_rwkv), jnp.stack(s_shift),
            jnp.stack(s_sbk), jnp.stack(s_sbv), jnp.stack(s_dn), jnp.stack(s_conv))
```

```python
import functools
import math

import jax
import jax.numpy as jnp
from jax import lax
from jax.experimental import pallas as pl
from jax.experimental.pallas import tpu as pltpu

F32 = jnp.float32
BF16 = jnp.bfloat16

LANES = 128
SUBLANES = 8
VMEM_LIMIT_BYTES = 56 * 1024 * 1024

EPS = 1e-6
PAGE_SIZE = 128

GLA_HEADS, GLA_DK, GLA_DV, GLA_LR = 4, 64, 128, 16
GLA_GATE_NORM = 16.0
GLA_CHUNK = 64
GLA_SUB = 16
RWKV_HEADS, RWKV_HEAD = 8, 64
RWKV_W_LR, RWKV_A_LR, RWKV_G_LR = 64, 64, 128
RWKV_LN_EPS = 64e-5
RWKV_CHUNK = 64
SB_HEADS, SB_HEAD = 8, 64
DN_HEADS, DN_HEAD, DN_CONV = 4, 128, 4
DN_CHUNK = 64
X_HEADS = 4


def _cparams(*sem):
    return pltpu.CompilerParams(dimension_semantics=sem, vmem_limit_bytes=VMEM_LIMIT_BYTES)


def _round_up(n, m):
    return -(-n // m) * m


def _pick_tile(n, candidates):
    for c in candidates:
        if n % c == 0:
            return c
    return n


def _dot(a, b):
    return jnp.dot(a.astype(BF16), b.astype(BF16), preferred_element_type=F32)


def _dot_nt(a, b):
    return lax.dot_general(a.astype(BF16), b.astype(BF16), (((1,), (1,)), ((), ())),
                           preferred_element_type=F32)


def _dot_tn(a, b):
    return lax.dot_general(a.astype(BF16), b.astype(BF16), (((0,), (0,)), ((), ())),
                           preferred_element_type=F32)


def _split3(a):
    hi = a.astype(BF16)
    r1 = a - hi.astype(F32)
    mid = r1.astype(BF16)
    lo = (r1 - mid.astype(F32)).astype(BF16)
    return hi, mid, lo


def _dot_x3(a, b):
    b = b.astype(BF16)
    hi, mid, lo = _split3(a)
    return (jnp.dot(hi, b, preferred_element_type=F32) + jnp.dot(mid, b, preferred_element_type=F32)
            + jnp.dot(lo, b, preferred_element_type=F32))


def _dot_3x(a, b):
    a = a.astype(BF16)
    hi, mid, lo = _split3(b)
    return (jnp.dot(a, hi, preferred_element_type=F32) + jnp.dot(a, mid, preferred_element_type=F32)
            + jnp.dot(a, lo, preferred_element_type=F32))


def _split2(a):
    hi = a.astype(BF16)
    return hi, (a - hi.astype(F32)).astype(BF16)


def _dot_hp(a, b):
    a_hi, a_lo = _split2(a)
    b_hi, b_lo = _split2(b)
    return (jnp.dot(a_hi, b_hi, preferred_element_type=F32)
            + (jnp.dot(a_hi, b_lo, preferred_element_type=F32) + jnp.dot(a_lo, b_hi, preferred_element_type=F32)))


def _iota(shape, dim):
    return lax.broadcasted_iota(jnp.int32, shape, dim)


def _group_ones(n, group):
    return (_iota((n, n), 0) // group == _iota((n, n), 1) // group).astype(F32)


def _group_sum(x, group):
    return _dot_x3(x, _group_ones(x.shape[-1], group))


def _rms(x, gain):
    return x * lax.rsqrt(jnp.mean(x * x, axis=-1, keepdims=True) + EPS) * gain


def _group_rms(x, gain_row, group):
    ms = _group_sum(x * x, group) * (1.0 / group)
    return x * lax.rsqrt(ms + EPS) * gain_row


def _silu(x):
    return x * jax.nn.sigmoid(x)


def _softplus(x):
    return jnp.maximum(x, 0.0) + jnp.log1p(jnp.exp(-jnp.abs(x)))


def _log_sigmoid(x):
    return -_softplus(-x)


def _norm_matmul_kernel(x_ref, g_ref, w_ref, *rest, head_group):
    if head_group:
        hg_ref, o_ref, h_ref = rest
    else:
        o_ref, h_ref = rest

    @pl.when(pl.program_id(1) == 0)
    def _():
        h_ref[...] = _rms(x_ref[...], g_ref[...]).astype(BF16)

    y = jnp.dot(h_ref[...], w_ref[...], preferred_element_type=F32)
    if head_group:
        y = _group_rms(y, hg_ref[...], head_group)
    o_ref[...] = y.astype(o_ref.dtype)


def norm_matmul(x, gain, w, *, head_gain=None, head_group=0, out_dtype=F32):
    m, k = x.shape
    n = w.shape[1]
    tm = _pick_tile(m, (1024, 512, 256, 128))
    tn = _pick_tile(n, (512, 384, 256, 128))
    in_specs = [pl.BlockSpec((tm, k), lambda i, j: (i, 0)),
                pl.BlockSpec((1, k), lambda i, j: (0, 0)),
                pl.BlockSpec((k, tn), lambda i, j: (0, j))]
    args = [x, gain.reshape(1, k), w]
    if head_group:
        in_specs.append(pl.BlockSpec((1, tn), lambda i, j: (0, j)))
        args.append(head_gain.reshape(1, n))
    return pl.pallas_call(
        functools.partial(_norm_matmul_kernel, head_group=head_group),
        grid=(m // tm, n // tn),
        in_specs=in_specs,
        out_specs=pl.BlockSpec((tm, tn), lambda i, j: (i, j)),
        out_shape=jax.ShapeDtypeStruct((m, n), out_dtype),
        scratch_shapes=[pltpu.VMEM((tm, k), BF16)],
        compiler_params=_cparams("parallel", "arbitrary"),
        name="norm_matmul",
    )(*args)


def _matmul_res_kernel(*refs, n_pairs):
    res_ref = refs[2 * n_pairs]
    o_ref = refs[2 * n_pairs + 1]
    acc = res_ref[...]
    for p in range(n_pairs):
        acc = acc + jnp.dot(refs[2 * p][...].astype(BF16), refs[2 * p + 1][...],
                            preferred_element_type=F32)
    o_ref[...] = acc


def matmul_residual(pairs, res):
    m, n = res.shape
    tm = _pick_tile(m, (1024, 512, 256, 128))
    tn = _pick_tile(n, (512, 256, 128))
    in_specs, args = [], []
    for a, w in pairs:
        kk = a.shape[1]
        in_specs.append(pl.BlockSpec((tm, kk), lambda i, j: (i, 0)))
        in_specs.append(pl.BlockSpec((kk, tn), lambda i, j: (0, j)))
        args += [a, w]
    in_specs.append(pl.BlockSpec((tm, tn), lambda i, j: (i, j)))
    args.append(res)
    return pl.pallas_call(
        functools.partial(_matmul_res_kernel, n_pairs=len(pairs)),
        grid=(m // tm, n // tn),
        in_specs=in_specs,
        out_specs=pl.BlockSpec((tm, tn), lambda i, j: (i, j)),
        out_shape=jax.ShapeDtypeStruct((m, n), F32),
        compiler_params=_cparams("parallel", "parallel"),
        name="matmul_residual",
    )(*args)


def _ffn_kernel(x_ref, g_ref, wg_ref, wu_ref, wd_ref, o_ref, h_ref, acc_ref):
    f = pl.program_id(1)

    @pl.when(f == 0)
    def _():
        x = x_ref[...]
        h_ref[...] = _rms(x, g_ref[...]).astype(BF16)
        acc_ref[...] = x

    h = h_ref[...]
    gate = jnp.dot(h, wg_ref[...], preferred_element_type=F32)
    up = jnp.dot(h, wu_ref[...], preferred_element_type=F32)
    act = (_silu(gate) * up).astype(BF16)
    acc_ref[...] += jnp.dot(act, wd_ref[...], preferred_element_type=F32)

    @pl.when(f == pl.num_programs(1) - 1)
    def _():
        o_ref[...] = acc_ref[...]


def ffn(x, gain, w_gu, w_down):
    m, d = x.shape
    ff = w_down.shape[0]
    tm = _pick_tile(m, (1024, 512, 256, 128))
    tf = _pick_tile(ff, (256, 128))
    nf = ff // tf
    return pl.pallas_call(
        _ffn_kernel,
        grid=(m // tm, nf),
        in_specs=[pl.BlockSpec((tm, d), lambda i, f: (i, 0)),
                  pl.BlockSpec((1, d), lambda i, f: (0, 0)),
                  pl.BlockSpec((d, tf), lambda i, f: (0, f)),
                  pl.BlockSpec((d, tf), lambda i, f: (0, f + nf)),
                  pl.BlockSpec((tf, d), lambda i, f: (f, 0))],
        out_specs=pl.BlockSpec((tm, d), lambda i, f: (i, 0)),
        out_shape=jax.ShapeDtypeStruct((m, d), F32),
        scratch_shapes=[pltpu.VMEM((tm, d), BF16), pltpu.VMEM((tm, d), F32)],
        compiler_params=_cparams("parallel", "arbitrary"),
        name="ffn",
    )(x, gain.reshape(1, d), w_gu, w_gu, w_down)


def _softmax_rows(s):
    p = jnp.exp(s - jnp.max(s, axis=-1, keepdims=True))
    return p / jnp.sum(p, axis=-1, keepdims=True)


def _cross_prompt_kernel(x_ref, g_ref, wq_ref, qg_ref, mk_ref, mv_ref, wo_ref, o_ref, *, heads):
    x = x_ref[...]
    d = x.shape[1]
    dh = d // heads
    h = _rms(x, g_ref[...]).astype(BF16)
    q = jnp.dot(h, wq_ref[...], preferred_element_type=F32)
    mk = mk_ref[...].astype(BF16)
    mv = mv_ref[...].astype(BF16)
    outs = []
    for hh in range(heads):
        sl = slice(hh * dh, (hh + 1) * dh)
        qh = _rms(q[:, sl], qg_ref[...])
        s = _dot_nt(qh, mk[:, sl]) * dh ** -0.5
        outs.append(_dot(_softmax_rows(s), mv[:, sl]))
    o = jnp.concatenate(outs, axis=-1)
    o_ref[...] = x + _dot(o, wo_ref[...])


def cross_attn_prompt(x, gain, w_xq, xq_norm, mk, mv, w_xo, *, batch):
    m, d = x.shape
    t = m // batch
    mem_len = mk.shape[0] // batch
    dh = d // X_HEADS
    tm = _pick_tile(t, (512, 256, 128))
    nt = t // tm
    return pl.pallas_call(
        functools.partial(_cross_prompt_kernel, heads=X_HEADS),
        grid=(batch, nt),
        in_specs=[pl.BlockSpec((tm, d), lambda b, i: (b * nt + i, 0)),
                  pl.BlockSpec((1, d), lambda b, i: (0, 0)),
                  pl.BlockSpec((d, d), lambda b, i: (0, 0)),
                  pl.BlockSpec((1, dh), lambda b, i: (0, 0)),
                  pl.BlockSpec((mem_len, d), lambda b, i: (b, 0)),
                  pl.BlockSpec((mem_len, d), lambda b, i: (b, 0)),
                  pl.BlockSpec((d, d), lambda b, i: (0, 0))],
        out_specs=pl.BlockSpec((tm, d), lambda b, i: (b * nt + i, 0)),
        out_shape=jax.ShapeDtypeStruct((m, d), F32),
        compiler_params=_cparams("parallel", "parallel"),
        name="cross_attn_prompt",
    )(x, gain.reshape(1, d), w_xq, xq_norm.reshape(1, dh), mk, mv, w_xo)


def _cross_sample_kernel(q_ref, mk_ref, mv_ref, o_ref, *, heads, mem_len):
    nb = q_ref.shape[0]
    d = q_ref.shape[2]
    dh = d // heads
    halves = dh // LANES
    assert heads * halves == SUBLANES
    row = _iota((SUBLANES, LANES), 0)
    for b in range(nb):
        q = q_ref[b] * dh ** -0.5
        s = jnp.zeros((SUBLANES, mem_len), F32)
        for r in range(SUBLANES):
            c, h = divmod(r, heads)
            chunk = q[:, h * dh + c * LANES:h * dh + (c + 1) * LANES]
            q_rows = jnp.where(row % heads == h, chunk, 0.0)
            s = s + _dot_nt(q_rows, mk_ref[0, b, pl.ds(r, mem_len, stride=SUBLANES), :])
        a = _softmax_rows(s)
        out = jnp.zeros((SUBLANES, LANES), F32)
        for r in range(SUBLANES):
            o_r = _dot(a, mv_ref[0, b, pl.ds(r, mem_len, stride=SUBLANES), :])
            out = out + jnp.where(row == r, o_r, 0.0)
        o_ref[b] = out


def cross_attn_sample(q, cache_k, cache_v, *, layer):
    db, d = q.shape
    depth, _, mem_len, heads, dh = cache_k.shape
    halves = dh // LANES

    def view(c):
        c = c.reshape(depth, db, mem_len, heads, halves, LANES)
        return c.transpose(0, 1, 2, 4, 3, 5).reshape(depth, db, mem_len * halves * heads, LANES)

    nb = _pick_tile(db, (4, 2, 1))
    rows = mem_len * halves * heads
    out = pl.pallas_call(
        functools.partial(_cross_sample_kernel, heads=heads, mem_len=mem_len),
        grid=(db // nb,),
        in_specs=[pl.BlockSpec((nb, 1, d), lambda i: (i, 0, 0)),
                  pl.BlockSpec((1, nb, rows, LANES), lambda i: (layer, i, 0, 0)),
                  pl.BlockSpec((1, nb, rows, LANES), lambda i: (layer, i, 0, 0))],
        out_specs=pl.BlockSpec((nb, halves * heads, LANES), lambda i: (i, 0, 0)),
        out_shape=jax.ShapeDtypeStruct((db, halves * heads, LANES), F32),
        compiler_params=_cparams("parallel"),
        name="cross_attn_sample",
    )(q.reshape(db, 1, d), view(cache_k), view(cache_v))
    return out.reshape(db, halves, heads, LANES).transpose(0, 2, 1, 3).reshape(db, d)


def _row_to_col(row):
    n = row.shape[1]
    eye = (_iota((n, n), 0) == _iota((n, n), 1)).astype(F32)
    return jnp.sum(eye * row, axis=-1, keepdims=True)


def _tri(n, strict=False):
    r, c = _iota((n, n), 0), _iota((n, n), 1)
    return ((r > c) if strict else (r >= c)).astype(F32)


GLA_PAIR = LANES // GLA_DK


def _gla_chunk_pairs(qs, ks, bs, vs, states):
    c = qs[0].shape[0]
    pairs = range(len(qs))
    sub = GLA_SUB
    nsub = c // sub
    lane = _iota((1, LANES), 1)
    masks = [(lane // GLA_DK == r).astype(F32) for r in range(GLA_PAIR)]
    ones = _group_ones(LANES, GLA_DK).astype(BF16)
    t_idx = _iota((c, LANES), 0)
    s_idx = _iota((c, LANES), 1) % c
    qe = [qs[p] * jnp.exp(bs[p]) for p in pairs]
    outs = [_dot(qe[p] * masks[r], states[p]) for p in pairs for r in range(GLA_PAIR)]
    rows = [[jnp.zeros((sub, LANES), F32)] for _ in pairs]
    for j in range(1, nsub):
        blk = slice(sub * j, sub * (j + 1))
        keep = _iota((sub, LANES), 1) % c < sub * j
        for p in pairs:
            ref = bs[p][sub * j - 1:sub * j, :]
            qj = qs[p][blk, :] * jnp.exp(bs[p][blk, :] - ref)
            kj = ks[p] * jnp.exp(jnp.minimum(ref - bs[p], 0.0))
            pj = jnp.concatenate([_dot_nt(qj * masks[r], kj) for r in range(GLA_PAIR)], axis=-1)
            rows[p].append(jnp.where(keep, pj, 0.0))
    pm = [jnp.concatenate(rows[p], axis=0) for p in pairs]
    for i in range(sub):
        hit = (s_idx == (t_idx // sub) * sub + i) & (t_idx % sub >= i)
        for p in pairs:
            ksel = jnp.concatenate(
                [jnp.broadcast_to(ks[p][sub * j + i:sub * j + i + 1, :], (sub, LANES)) for j in range(nsub)], axis=0)
            bsel = jnp.concatenate(
                [jnp.broadcast_to(bs[p][sub * j + i:sub * j + i + 1, :], (sub, LANES)) for j in range(nsub)], axis=0)
            hi, lo = _split2(qs[p] * ksel * jnp.exp(jnp.minimum(bs[p] - bsel, 0.0)))
            col = jnp.dot(hi, ones, preferred_element_type=F32) + jnp.dot(lo, ones, preferred_element_type=F32)
            pm[p] = jnp.where(hit, col, pm[p])
    outs = [outs[p * GLA_PAIR + r]
            + _dot(pm[p] * masks[r], jnp.concatenate([vs[p * GLA_PAIR + r]] * GLA_PAIR, axis=0))
            for p in pairs for r in range(GLA_PAIR)]
    lasts = [b[c - 1:c, :] for b in bs]
    kd = [ks[p] * jnp.exp(lasts[p] - bs[p]) for p in pairs]
    new = [_row_to_col(jnp.exp(lasts[p])) * states[p]
           + sum(_dot_tn(kd[p] * masks[r], vs[p * GLA_PAIR + r]) for r in range(GLA_PAIR)) for p in pairs]
    return outs, new


def _gla_prompt_kernel(g_ref, wa_ref, ba_ref, gn_ref, o_ref, sout_ref, s_ref):
    tb = g_ref.shape[0]
    c = GLA_CHUNK
    nk = GLA_HEADS * GLA_DK
    half = GLA_HEADS * GLA_DV
    lr_off = 2 * nk + 2 * half
    n_pairs = GLA_HEADS // GLA_PAIR
    assert GLA_PAIR * c == LANES

    @pl.when(pl.program_id(1) == 0)
    def _():
        s_ref[...] = jnp.zeros_like(s_ref)

    def chunk(ci, carry):
        rows = pl.ds(pl.multiple_of(ci * c, c), c)
        lr = g_ref[rows, lr_off:lr_off + LANES]
        lf = _log_sigmoid(_dot(lr, wa_ref[...]) + ba_ref[...]) * (1.0 / GLA_GATE_NORM)
        bcum = _dot_3x(_tri(c), lf)
        outs, new = _gla_chunk_pairs(
            [g_ref[rows, p * LANES:(p + 1) * LANES] * GLA_DK ** -0.5 for p in range(n_pairs)],
            [g_ref[rows, nk + p * LANES:nk + (p + 1) * LANES] for p in range(n_pairs)],
            [bcum[:, p * LANES:(p + 1) * LANES] for p in range(n_pairs)],
            [g_ref[rows, 2 * nk + h * GLA_DV:2 * nk + (h + 1) * GLA_DV] for h in range(GLA_HEADS)],
            [s_ref[p] for p in range(n_pairs)])
        for p in range(n_pairs):
            s_ref[p] = new[p]
        for h in range(GLA_HEADS):
            gate = g_ref[rows, 2 * nk + half + h * GLA_DV:2 * nk + half + (h + 1) * GLA_DV]
            o_ref[rows, h * GLA_DV:(h + 1) * GLA_DV] = (_rms(outs[h], gn_ref[...]) * _silu(gate)).astype(o_ref.dtype)
        return carry

    lax.fori_loop(0, tb // c, chunk, 0)
    sout_ref[0] = s_ref[...]


def gla_prompt(g_cols, w_a2, b_a, gnorm, *, batch):
    m, width = g_cols.shape
    t = m // batch
    tb = _pick_tile(t, (256, 128, 64))
    nt = t // tb
    nk = GLA_HEADS * GLA_DK
    half = GLA_HEADS * GLA_DV
    wa = jnp.zeros((LANES, nk), F32).at[:GLA_LR].set(w_a2)
    n_pairs = GLA_HEADS // GLA_PAIR
    o, s_pairs = pl.pallas_call(
        _gla_prompt_kernel,
        grid=(batch, nt),
        in_specs=[pl.BlockSpec((tb, width), lambda b, i: (b * nt + i, 0)),
                  pl.BlockSpec((LANES, nk), lambda b, i: (0, 0)),
                  pl.BlockSpec((1, nk), lambda b, i: (0, 0)),
                  pl.BlockSpec((1, GLA_DV), lambda b, i: (0, 0))],
        out_specs=[pl.BlockSpec((tb, half), lambda b, i: (b * nt + i, 0)),
                   pl.BlockSpec((1, n_pairs, LANES, GLA_DV), lambda b, i: (b, 0, 0, 0))],
        out_shape=[jax.ShapeDtypeStruct((m, half), BF16),
                   jax.ShapeDtypeStruct((batch, n_pairs, LANES, GLA_DV), F32)],
        scratch_shapes=[pltpu.VMEM((n_pairs, LANES, GLA_DV), F32)],
        compiler_params=_cparams("parallel", "arbitrary"),
        name="gla_prompt",
    )(g_cols, wa, b_a.reshape(1, nk), gnorm.reshape(1, GLA_DV))
    return o, s_pairs.reshape(batch, GLA_HEADS, GLA_DK, GLA_DV)


RWKV_MIX = RWKV_HEADS * RWKV_HEAD
RWKV_COLS = 3 * RWKV_MIX + RWKV_W_LR + RWKV_A_LR + RWKV_G_LR
RWKV_PARAM_ROWS = 8


def _unit_lower_inverses(mats):
    c = mats[0].shape[0]
    eye = (_iota((c, c), 0) == _iota((c, c), 1)).astype(F32)
    prods = [eye + m for m in mats]
    powers = list(mats)
    span = 2
    while span < c:
        powers = [_dot_hp(m, m) for m in powers]
        prods = [p + _dot_hp(p, m) for p, m in zip(prods, powers)]
        span *= 2
    return prods


def _rwkv_prep(r, prev, mu, lora_w, lora_a, g2, tab):
    m = RWKV_MIX
    xr = r + (prev - r) * mu
    rr, rk, rv = xr[:, :m], xr[:, m:2 * m], xr[:, 2 * m:3 * m]
    wa_in = xr[:, 3 * m:3 * m + LANES]
    wa_in = jnp.where(_iota(wa_in.shape, 1) < RWKV_W_LR, jnp.tanh(wa_in), wa_in)
    rg = xr[:, 3 * m + LANES:3 * m + 2 * LANES]
    w0, a0, k_k, k_a = tab[0:1], tab[1:2], tab[2:3], tab[3:4]
    w_log = -_softplus(-(w0 + _dot(wa_in, lora_w))) - 0.5
    log_decay = -jnp.exp(w_log)
    a = jax.nn.sigmoid(a0 + _dot(wa_in, lora_a))
    g = _dot(jax.nn.sigmoid(rg), g2)
    kx = rk * k_k
    kk = kx * lax.rsqrt(_group_sum(kx * kx, RWKV_HEAD) + EPS)
    k2 = rk * (1.0 + (a - 1.0) * k_a)
    return rr, k2, rv, kk, a, log_decay, g


def _rwkv_finish(y, rr, k2, rv, g, tab):
    r_k, ln_w, ln_b = tab[4:5], tab[5:6], tab[6:7]
    inv = 1.0 / RWKV_HEAD
    mean = _group_sum(y, RWKV_HEAD) * inv
    yc = y - mean
    var = _group_sum(yc * yc, RWKV_HEAD) * inv
    yn = yc * lax.rsqrt(var + RWKV_LN_EPS) * ln_w + ln_b
    bonus = _group_sum(rr * k2 * r_k, RWKV_HEAD) * rv
    return (yn + bonus) * g


def _rwkv_chunk_heads(at, rt, bt, kt, bh, kh, v, decay_c, states):
    c = at.shape[0]
    hd = RWKV_HEAD
    heads = range(len(states))
    sls = [slice(h * hd, (h + 1) * hd) for h in heads]
    strict = _tri(c, strict=True)
    incl = _tri(c)
    mms = [_dot_nt(jnp.concatenate([at[:, sl], rt[:, sl]], axis=0),
                   jnp.concatenate([bt[:, sl], kt[:, sl]], axis=0)) for sl in sls]
    t_invs = _unit_lower_inverses([mm[:c, :c] * strict for mm in mms])
    xs = [_dot_nt(at[:, sls[h]], states[h]) + _dot(mms[h][:c, c:] * strict, v[:, sls[h]]) for h in heads]
    us = [_dot(t_invs[h], xs[h]) for h in heads]
    ys = [_dot_nt(rt[:, sls[h]], states[h]) + _dot(mms[h][c:, :c] * incl, us[h])
          + _dot(mms[h][c:, c:] * incl, v[:, sls[h]]) for h in heads]
    new = [states[h] * decay_c[:, sls[h]] + _dot_tn(us[h], bh[:, sls[h]]) + _dot_tn(v[:, sls[h]], kh[:, sls[h]])
           for h in heads]
    return ys, new


def _rwkv_prompt_kernel(r_ref, mu_ref, lw_ref, la_ref, g2_ref, tab_ref, o_ref, sout_ref, shout_ref,
                        s_ref, shift_ref, y_ref, f_ref):
    tb = r_ref.shape[0]
    c = RWKV_CHUNK
    hd = RWKV_HEAD

    @pl.when(pl.program_id(1) == 0)
    def _():
        s_ref[...] = jnp.zeros_like(s_ref)
        shift_ref[...] = jnp.zeros_like(shift_ref)

    r = r_ref[...]
    prev = jnp.where(_iota(r.shape, 0) == 0, shift_ref[...], pltpu.roll(r, 1, axis=0))
    shift_ref[...] = r[tb - 1:tb, :]
    tab = tab_ref[...]
    rr, k2, rv, kk, a, ld, g = _rwkv_prep(r, prev, mu_ref[...], lw_ref[...], la_ref[...], g2_ref[...], tab)
    f_ref[0] = rr
    f_ref[1] = k2
    f_ref[2] = rv
    f_ref[3] = kk
    f_ref[4] = a
    f_ref[5] = ld

    def chunk(ci, carry):
        rows = pl.ds(pl.multiple_of(ci * c, c), c)
        rr_c, k2_c, rv_c, kk_c, a_c, ld_c = (f_ref[i, rows, :] for i in range(6))
        gcum = _dot_3x(_tri(c), ld_c)
        g_end = gcum[c - 1:c, :]
        e_neg = jnp.exp(-gcum)
        e_end = jnp.exp(g_end - gcum)
        beta = kk_c * a_c
        at = -kk_c * jnp.exp(gcum - ld_c)
        rt = rr_c * jnp.exp(gcum)
        bt = beta * e_neg
        kt = k2_c * e_neg
        bh = beta * e_end
        kh = k2_c * e_end
        decay_c = jnp.exp(g_end)
        ys, new = _rwkv_chunk_heads(at, rt, bt, kt, bh, kh, rv_c, decay_c,
                                    [s_ref[h] for h in range(RWKV_HEADS)])
        for h in range(RWKV_HEADS):
            s_ref[h] = new[h]
            y_ref[rows, h * hd:(h + 1) * hd] = ys[h]
        return carry

    lax.fori_loop(0, tb // c, chunk, 0)
    o_ref[...] = _rwkv_finish(y_ref[...], rr, k2, rv, g, tab).astype(o_ref.dtype)
    sout_ref[0] = s_ref[...]
    shout_ref[0] = shift_ref[...]


def _rwkv_params(mu, w0, w2, a0, a2, g2, k_k, k_a, r_k, ln_w, ln_b):
    m = RWKV_MIX
    lora_w = jnp.zeros((LANES, m), F32).at[:RWKV_W_LR].set(w2)
    lora_a = jnp.zeros((LANES, m), F32).at[RWKV_W_LR:RWKV_W_LR + RWKV_A_LR].set(a2)
    tab = jnp.stack([w0, a0, k_k, k_a, r_k, ln_w, ln_b, jnp.zeros_like(w0)])
    return mu.reshape(1, RWKV_COLS), lora_w, lora_a, g2, tab


def rwkv_prompt(r_cols, params, *, batch):
    m, width = r_cols.shape
    t = m // batch
    tb = _pick_tile(t, (256, 128, 64))
    nt = t // tb
    mix = RWKV_MIX
    mu, lora_w, lora_a, g2, tab = params
    const = lambda b, i: (0, 0)
    return pl.pallas_call(
        _rwkv_prompt_kernel,
        grid=(batch, nt),
        in_specs=[pl.BlockSpec((tb, width), lambda b, i: (b * nt + i, 0)),
                  pl.BlockSpec((1, width), const),
                  pl.BlockSpec((LANES, mix), const),
                  pl.BlockSpec((LANES, mix), const),
                  pl.BlockSpec((RWKV_G_LR, mix), const),
                  pl.BlockSpec((RWKV_PARAM_ROWS, mix), const)],
        out_specs=[pl.BlockSpec((tb, mix), lambda b, i: (b * nt + i, 0)),
                   pl.BlockSpec((1, RWKV_HEADS, RWKV_HEAD, RWKV_HEAD), lambda b, i: (b, 0, 0, 0)),
                   pl.BlockSpec((1, 1, width), lambda b, i: (b, 0, 0))],
        out_shape=[jax.ShapeDtypeStruct((m, mix), BF16),
                   jax.ShapeDtypeStruct((batch, RWKV_HEADS, RWKV_HEAD, RWKV_HEAD), F32),
                   jax.ShapeDtypeStruct((batch, 1, width), F32)],
        scratch_shapes=[pltpu.VMEM((RWKV_HEADS, RWKV_HEAD, RWKV_HEAD), F32),
                        pltpu.VMEM((1, width), F32),
                        pltpu.VMEM((tb, mix), F32),
                        pltpu.VMEM((6, tb, mix), F32)],
        compiler_params=_cparams("parallel", "arbitrary"),
        name="rwkv_prompt",
    )(r_cols, mu, lora_w, lora_a, g2, tab)


DN_MIX = DN_HEADS * DN_HEAD
DN_CONV_CH = 3 * DN_MIX
DN_GATE_OFF = DN_CONV_CH + DN_MIX


def _col_to_row(col):
    n = col.shape[0]
    eye = (_iota((n, n), 0) == _iota((n, n), 1)).astype(F32)
    return jnp.sum(eye * col, axis=0, keepdims=True)


def _dn_gates(gb_tile, a_log_row, dt_bias_row):
    gdec = -jnp.exp(a_log_row) * _softplus(gb_tile + dt_bias_row)
    beta = jax.nn.sigmoid(gb_tile)
    return gdec, beta


def _l2n(x):
    return x * lax.rsqrt(jnp.sum(x * x, axis=-1, keepdims=True) + EPS)


def _dn_chunk_setup(qs, ks, vs, g_cols, beta_cols):
    c = qs[0].shape[0]
    pairs = range(len(qs))
    incl = _tri(c)
    strict = _tri(c, strict=True)
    dmats = [jnp.exp(jnp.minimum(g - _col_to_row(g), 0.0)) * incl for g in g_cols]
    kks = [_dot_nt(k, k) for k in ks]
    t_invs = _unit_lower_inverses([-(strict * beta_cols[n] * kks[n] * dmats[n]) for n in pairs])
    tvs = [_dot(t_invs[n], beta_cols[n] * vs[n]) for n in pairs]
    tks = [_dot(t_invs[n], (beta_cols[n] * jnp.exp(g_cols[n])) * ks[n]) for n in pairs]
    qks = [_dot_nt(qs[n], ks[n]) * dmats[n] for n in pairs]
    lasts = [g[c - 1:c, :] for g in g_cols]
    return [(tvs[n], tks[n], qks[n], qs[n] * jnp.exp(g_cols[n]), ks[n] * jnp.exp(lasts[n] - g_cols[n]),
             jnp.exp(lasts[n])) for n in pairs]


def _dn_chunk_scan(parts, states):
    heads = range(len(states))
    us = [parts[h][0] - _dot(parts[h][1], states[h]) for h in heads]
    outs = [_dot(parts[h][3], states[h]) + _dot(parts[h][2], us[h]) for h in heads]
    new = [parts[h][5] * states[h] + _dot_tn(parts[h][4], us[h]) for h in heads]
    return outs, new


def _shift_rows(x, k, before):
    out = pltpu.roll(x, k, axis=0)
    row = _iota(x.shape, 0)
    nb = DN_CONV - 1
    for r in range(k):
        out = jnp.where(row == r, before[nb - k + r:nb - k + r + 1, :], out)
    return out


def _dn_prompt_kernel(c_ref, cw_ref, al_ref, dtb_ref, gn_ref, o_ref, sout_ref, cout_ref,
                      s_ref, buf_ref, x_ref, g_ref, p_ref, qk_ref, dl_ref):
    tb = c_ref.shape[0]
    c = DN_CHUNK
    hd = DN_HEAD
    nb = DN_CONV - 1

    @pl.when(pl.program_id(1) == 0)
    def _():
        s_ref[...] = jnp.zeros_like(s_ref)
        buf_ref[...] = jnp.zeros_like(buf_ref)

    x = c_ref[:, :DN_CONV_CH]
    before = buf_ref[...]
    cw = cw_ref[...]
    y = x * cw[nb:nb + 1, :]
    for k in range(1, DN_CONV):
        y = y + _shift_rows(x, k, before) * cw[nb - k:nb - k + 1, :]
    buf_ref[...] = x[tb - nb:tb, :]
    xc = _silu(y)
    for h in range(DN_HEADS):
        x_ref[:, h * hd:(h + 1) * hd] = _l2n(xc[:, h * hd:(h + 1) * hd]) * hd ** -0.5
        x_ref[:, DN_MIX + h * hd:DN_MIX + (h + 1) * hd] = _l2n(xc[:, DN_MIX + h * hd:DN_MIX + (h + 1) * hd])
    x_ref[:, 2 * DN_MIX:] = xc[:, 2 * DN_MIX:]
    gdec, beta = _dn_gates(c_ref[:, DN_GATE_OFF:DN_GATE_OFF + LANES], al_ref[...], dtb_ref[...])
    g_ref[0] = gdec
    g_ref[1] = beta

    heads = range(DN_HEADS)
    n_chunks = tb // c
    group = 2 if n_chunks % 2 == 0 else 1

    def setup(gi, carry):
        qs, ks, vs, gs, bs = [], [], [], [], []
        for j in range(group):
            rows = pl.ds(pl.multiple_of((gi * group + j) * c, c), c)
            gcum = _dot_3x(_tri(c), g_ref[0, rows, :])
            beta_c = g_ref[1, rows, :]
            qs += [x_ref[rows, h * hd:(h + 1) * hd] for h in heads]
            ks += [x_ref[rows, DN_MIX + h * hd:DN_MIX + (h + 1) * hd] for h in heads]
            vs += [x_ref[rows, 2 * DN_MIX + h * hd:2 * DN_MIX + (h + 1) * hd] for h in heads]
            gs += [gcum[:, h:h + 1] for h in heads]
            bs += [beta_c[:, DN_HEADS + h:DN_HEADS + h + 1] for h in heads]
        parts = _dn_chunk_setup(qs, ks, vs, gs, bs)
        for j in range(group):
            for h in heads:
                tv, tk, qk, qg, kd, dl = parts[j * DN_HEADS + h]
                ci = gi * group + j
                p_ref[0, ci, h] = tv
                p_ref[1, ci, h] = tk
                p_ref[2, ci, h] = qg
                p_ref[3, ci, h] = kd
                qk_ref[ci, h] = qk
                dl_ref[ci, h] = jnp.broadcast_to(dl, (1, hd))
        return carry

    lax.fori_loop(0, n_chunks // group, setup, 0)

    def scan(ci, carry):
        rows = pl.ds(pl.multiple_of(ci * c, c), c)
        parts = [(p_ref[0, ci, h], p_ref[1, ci, h], qk_ref[ci, h], p_ref[2, ci, h], p_ref[3, ci, h], dl_ref[ci, h])
                 for h in heads]
        outs, new = _dn_chunk_scan(parts, [s_ref[h] for h in heads])
        for h in heads:
            s_ref[h] = new[h]
            z = c_ref[rows, DN_CONV_CH + h * hd:DN_CONV_CH + (h + 1) * hd]
            o_ref[rows, h * hd:(h + 1) * hd] = (_rms(outs[h], gn_ref[...]) * _silu(z)).astype(o_ref.dtype)
        return carry

    lax.fori_loop(0, n_chunks, scan, 0)
    sout_ref[0] = s_ref[...]
    cout_ref[0] = buf_ref[...]


def _dn_params(conv_w, a_log, dt_bias, gnorm):
    al = jnp.zeros((1, LANES), F32).at[0, :DN_HEADS].set(a_log)
    dtb = jnp.zeros((1, LANES), F32).at[0, :DN_HEADS].set(dt_bias)
    return conv_w, al, dtb, gnorm.reshape(1, DN_HEAD)


def dn_prompt(d_cols, params, *, batch):
    m, width = d_cols.shape
    t = m // batch
    tb = _pick_tile(t, (256, 128, 64))
    nt = t // tb
    conv_w, al, dtb, gn = params
    const = lambda b, i: (0, 0)
    nb = DN_CONV - 1
    return pl.pallas_call(
        _dn_prompt_kernel,
        grid=(batch, nt),
        in_specs=[pl.BlockSpec((tb, width), lambda b, i: (b * nt + i, 0)),
                  pl.BlockSpec((DN_CONV, DN_CONV_CH), const),
                  pl.BlockSpec((1, LANES), const),
                  pl.BlockSpec((1, LANES), const),
                  pl.BlockSpec((1, DN_HEAD), const)],
        out_specs=[pl.BlockSpec((tb, DN_MIX), lambda b, i: (b * nt + i, 0)),
                   pl.BlockSpec((1, DN_HEADS, DN_HEAD, DN_HEAD), lambda b, i: (b, 0, 0, 0)),
                   pl.BlockSpec((1, nb, DN_CONV_CH), lambda b, i: (b, 0, 0))],
        out_shape=[jax.ShapeDtypeStruct((m, DN_MIX), BF16),
                   jax.ShapeDtypeStruct((batch, DN_HEADS, DN_HEAD, DN_HEAD), F32),
                   jax.ShapeDtypeStruct((batch, nb, DN_CONV_CH), F32)],
        scratch_shapes=[pltpu.VMEM((DN_HEADS, DN_HEAD, DN_HEAD), F32),
                        pltpu.VMEM((nb, DN_CONV_CH), F32),
                        pltpu.VMEM((tb, DN_CONV_CH), F32),
                        pltpu.VMEM((2, tb, LANES), F32),
                        pltpu.VMEM((4, tb // DN_CHUNK, DN_HEADS, DN_CHUNK, DN_HEAD), F32),
                        pltpu.VMEM((tb // DN_CHUNK, DN_HEADS, DN_CHUNK, DN_CHUNK), F32),
                        pltpu.VMEM((tb // DN_CHUNK, DN_HEADS, 1, DN_HEAD), F32)],
        compiler_params=_cparams("parallel", "arbitrary"),
        name="dn_prompt",
    )(d_cols, conv_w, al, dtb, gn)


SB_MIX = SB_HEADS * SB_HEAD
SB_PAIR = LANES // SB_HEAD


def _sb_tiles(zs, causal, carries, v, v_transposed=False):
    tk = zs[0].shape[1]
    heads = range(len(zs))
    upper = (_iota((tk, tk), 0) > _iota((tk, tk), 1)).astype(BF16)
    stay = [jnp.minimum(-z, 0.0) - jnp.log(1.0 + jnp.exp(-jnp.abs(z))) for z in zs]
    if causal is not None:
        stay = [jnp.where(causal, x, 0.0) for x in stay]
    parts = [_split2(x) for x in stay]
    later = [jnp.dot(hi, upper, preferred_element_type=F32) + jnp.dot(lo, upper, preferred_element_type=F32)
             for hi, lo in parts]
    ws = [jnp.exp(zs[h] + stay[h] + (later[h] + carries[h])) for h in heads]
    if causal is not None:
        ws = [jnp.where(causal, w, 0.0) for w in ws]
    pvs = [(_dot_nt if v_transposed else _dot)(w, v) for w in ws]
    return pvs, [carries[h] + jnp.sum(stay[h], axis=-1, keepdims=True) for h in heads]


def _sb_prompt_kernel(q_ref, k_ref, v_ref, b_ref, o_ref):
    tq = q_ref.shape[0]
    i = pl.program_id(2)
    lane = _iota((1, LANES), 1)
    masks = [(lane // SB_HEAD == r).astype(F32) for r in range(SB_PAIR)]
    q = q_ref[...] * SB_HEAD ** -0.5
    qm = [(q * masks[r]).astype(BF16) for r in range(SB_PAIR)]
    bias = [b_ref[0, r:r + 1, 0:1] for r in range(SB_PAIR)]

    def tile(j, causal, state):
        rows = pl.ds(pl.multiple_of(j * tq, tq), tq)
        k = k_ref[rows, :]
        zs = [_dot_nt(qm[r], k) + bias[r] for r in range(SB_PAIR)]
        pvs, carries = _sb_tiles(zs, causal, [state[r][1] for r in range(SB_PAIR)], v_ref[rows, :])
        return tuple((state[r][0] + pvs[r], carries[r]) for r in range(SB_PAIR))

    init = tuple((jnp.zeros((tq, LANES), F32), jnp.zeros((tq, 1), F32)) for _ in range(SB_PAIR))
    state = tile(i, _iota((tq, tq), 1) < _iota((tq, tq), 0), init)
    state = lax.fori_loop(0, i, lambda step, st: tile(i - 1 - step, None, st), state)
    o_ref[...] = sum(state[r][0] * masks[r] for r in range(SB_PAIR)).astype(o_ref.dtype)


def sb_prompt(q, k, v, bias, *, batch):
    m, width = q.shape
    t = m // batch
    tq = _pick_tile(t, (256, 128))
    nq = t // tq
    npair = SB_HEADS // SB_PAIR
    bias_rows = jnp.broadcast_to(bias.reshape(npair, SB_PAIR, 1), (npair, SB_PAIR, LANES))
    return pl.pallas_call(
        _sb_prompt_kernel,
        grid=(batch, npair, nq),
        in_specs=[pl.BlockSpec((tq, LANES), lambda b, p, i: (b * nq + i, p)),
                  pl.BlockSpec((t, LANES), lambda b, p, i: (b, p)),
                  pl.BlockSpec((t, LANES), lambda b, p, i: (b, p)),
                  pl.BlockSpec((1, SB_PAIR, LANES), lambda b, p, i: (p, 0, 0))],
        out_specs=pl.BlockSpec((tq, LANES), lambda b, p, i: (b * nq + i, p)),
        out_shape=jax.ShapeDtypeStruct((m, width), BF16),
        compiler_params=_cparams("parallel", "parallel", "parallel"),
        name="sb_prompt",
    )(q, k, v, bias_rows)


STEP_ROWS = 8


def _gla_step_kernel(g_ref, wa_ref, ba_ref, gn_ref, s_ref, o_ref, sout_ref):
    nk = GLA_HEADS * GLA_DK
    half = GLA_HEADS * GLA_DV
    lr_off = 2 * nk + 2 * half
    lf = _log_sigmoid(_dot(g_ref[:, lr_off:lr_off + LANES], wa_ref[...]) + ba_ref[...]) * (1.0 / GLA_GATE_NORM)
    decay = jnp.exp(lf)
    for b in range(g_ref.shape[0]):
        for h in range(GLA_HEADS):
            q_col = _row_to_col(g_ref[b:b + 1, h * GLA_DK:(h + 1) * GLA_DK] * GLA_DK ** -0.5)
            k_col = _row_to_col(g_ref[b:b + 1, nk + h * GLA_DK:nk + (h + 1) * GLA_DK])
            f_col = _row_to_col(decay[b:b + 1, h * GLA_DK:(h + 1) * GLA_DK])
            v = g_ref[b:b + 1, 2 * nk + h * GLA_DV:2 * nk + (h + 1) * GLA_DV]
            gate = g_ref[b:b + 1, 2 * nk + half + h * GLA_DV:2 * nk + half + (h + 1) * GLA_DV]
            s_new = f_col * s_ref[0, b, h] + k_col * v
            sout_ref[b, h] = s_new
            o = jnp.sum(q_col * s_new, axis=0, keepdims=True)
            o_ref[b:b + 1, h * GLA_DV:(h + 1) * GLA_DV] = _rms(o, gn_ref[...]) * _silu(gate)


def gla_step(g_cols, w_a2, b_a, gnorm, state, *, layer):
    db, width = g_cols.shape
    nk = GLA_HEADS * GLA_DK
    half = GLA_HEADS * GLA_DV
    wa = jnp.zeros((LANES, nk), F32).at[:GLA_LR].set(w_a2)
    nb = STEP_ROWS
    return pl.pallas_call(
        _gla_step_kernel,
        grid=(db // nb,),
        in_specs=[pl.BlockSpec((nb, width), lambda i: (i, 0)),
                  pl.BlockSpec((LANES, nk), lambda i: (0, 0)),
                  pl.BlockSpec((1, nk), lambda i: (0, 0)),
                  pl.BlockSpec((1, GLA_DV), lambda i: (0, 0)),
                  pl.BlockSpec((1, nb, GLA_HEADS, GLA_DK, GLA_DV), lambda i: (layer, i, 0, 0, 0))],
        out_specs=[pl.BlockSpec((nb, half), lambda i: (i, 0)),
                   pl.BlockSpec((nb, GLA_HEADS, GLA_DK, GLA_DV), lambda i: (i, 0, 0, 0))],
        out_shape=[jax.ShapeDtypeStruct((db, half), F32),
                   jax.ShapeDtypeStruct((db, GLA_HEADS, GLA_DK, GLA_DV), F32)],
        compiler_params=_cparams("parallel"),
        name="gla_step",
    )(g_cols, wa, b_a.reshape(1, nk), gnorm.reshape(1, GLA_DV), state)


def _rwkv_step_kernel(r_ref, prev_ref, mu_ref, lw_ref, la_ref, g2_ref, tab_ref, s_ref, o_ref, sout_ref, y_ref):
    hd = RWKV_HEAD
    tab = tab_ref[...]
    rr, k2, rv, kk, a, ld, g = _rwkv_prep(r_ref[...], prev_ref[0], mu_ref[...], lw_ref[...], la_ref[...],
                                          g2_ref[...], tab)
    w = jnp.exp(ld)
    kka = kk * a
    for b in range(r_ref.shape[0]):
        for h in range(RWKV_HEADS):
            sl = slice(h * hd, (h + 1) * hd)
            s = s_ref[0, b, h]
            sa_col = jnp.sum(s * -kk[b:b + 1, sl], axis=1, keepdims=True)
            s_new = s * w[b:b + 1, sl] + sa_col * kka[b:b + 1, sl] + _row_to_col(rv[b:b + 1, sl]) * k2[b:b + 1, sl]
            sout_ref[b, h] = s_new
            y_col = jnp.sum(s_new * rr[b:b + 1, sl], axis=1, keepdims=True)
            y_ref[b:b + 1, sl] = _col_to_row(y_col)
    o_ref[...] = _rwkv_finish(y_ref[...], rr, k2, rv, g, tab)


def rwkv_step(r_cols, params, state, shift, *, layer):
    db, width = r_cols.shape
    mix = RWKV_MIX
    mu, lora_w, lora_a, g2, tab = params
    nb = STEP_ROWS
    const = lambda i: (0, 0)
    return pl.pallas_call(
        _rwkv_step_kernel,
        grid=(db // nb,),
        in_specs=[pl.BlockSpec((nb, width), lambda i: (i, 0)),
                  pl.BlockSpec((1, nb, width), lambda i: (layer, i, 0)),
                  pl.BlockSpec((1, width), const),
                  pl.BlockSpec((LANES, mix), const),
                  pl.BlockSpec((LANES, mix), const),
                  pl.BlockSpec((RWKV_G_LR, mix), const),
                  pl.BlockSpec((RWKV_PARAM_ROWS, mix), const),
                  pl.BlockSpec((1, nb, RWKV_HEADS, RWKV_HEAD, RWKV_HEAD), lambda i: (layer, i, 0, 0, 0))],
        out_specs=[pl.BlockSpec((nb, mix), lambda i: (i, 0)),
                   pl.BlockSpec((nb, RWKV_HEADS, RWKV_HEAD, RWKV_HEAD), lambda i: (i, 0, 0, 0))],
        out_shape=[jax.ShapeDtypeStruct((db, mix), F32),
                   jax.ShapeDtypeStruct((db, RWKV_HEADS, RWKV_HEAD, RWKV_HEAD), F32)],
        scratch_shapes=[pltpu.VMEM((nb, mix), F32)],
        compiler_params=_cparams("parallel"),
        name="rwkv_step",
    )(r_cols, shift, mu, lora_w, lora_a, g2, tab, state)


def _dn_step_kernel(c_ref, cw_ref, al_ref, dtb_ref, gn_ref, s_ref, buf_ref, o_ref, sout_ref, cout_ref):
    hd = DN_HEAD
    nb = DN_CONV - 1
    cw = cw_ref[...]
    gdec, beta = _dn_gates(c_ref[:, DN_GATE_OFF:DN_GATE_OFF + LANES], al_ref[...], dtb_ref[...])
    decay = jnp.exp(gdec)
    for b in range(c_ref.shape[0]):
        x = c_ref[b:b + 1, :DN_CONV_CH]
        before = buf_ref[0, b]
        y = x * cw[nb:nb + 1, :]
        for k in range(nb):
            y = y + before[k:k + 1, :] * cw[k:k + 1, :]
        cout_ref[b] = jnp.concatenate([before[1:nb, :], x], axis=0)
        xc = _silu(y)
        for h in range(DN_HEADS):
            q = _l2n(xc[:, h * hd:(h + 1) * hd]) * hd ** -0.5
            k = _l2n(xc[:, DN_MIX + h * hd:DN_MIX + (h + 1) * hd])
            v = xc[:, 2 * DN_MIX + h * hd:2 * DN_MIX + (h + 1) * hd]
            dec = decay[b:b + 1, h:h + 1]
            bet = beta[b:b + 1, DN_HEADS + h:DN_HEADS + h + 1]
            s = s_ref[0, b, h]
            k_col = _row_to_col(k)
            u = bet * (v - dec * jnp.sum(k_col * s, axis=0, keepdims=True))
            s_new = dec * s + k_col * u
            sout_ref[b, h] = s_new
            o = jnp.sum(_row_to_col(q) * s_new, axis=0, keepdims=True)
            z = c_ref[b:b + 1, DN_CONV_CH + h * hd:DN_CONV_CH + (h + 1) * hd]
            o_ref[b:b + 1, h * hd:(h + 1) * hd] = _rms(o, gn_ref[...]) * _silu(z)


def dn_step(d_cols, params, state, conv_buf, *, layer):
    db, width = d_cols.shape
    conv_w, al, dtb, gn = params
    nb = STEP_ROWS
    nc = DN_CONV - 1
    const = lambda i: (0, 0)
    return pl.pallas_call(
        _dn_step_kernel,
        grid=(db // nb,),
        in_specs=[pl.BlockSpec((nb, width), lambda i: (i, 0)),
                  pl.BlockSpec((DN_CONV, DN_CONV_CH), const),
                  pl.BlockSpec((1, LANES), const),
                  pl.BlockSpec((1, LANES), const),
                  pl.BlockSpec((1, DN_HEAD), const),
                  pl.BlockSpec((1, nb, DN_HEADS, DN_HEAD, DN_HEAD), lambda i: (layer, i, 0, 0, 0)),
                  pl.BlockSpec((1, nb, nc, DN_CONV_CH), lambda i: (layer, i, 0, 0))],
        out_specs=[pl.BlockSpec((nb, DN_MIX), lambda i: (i, 0)),
                   pl.BlockSpec((nb, DN_HEADS, DN_HEAD, DN_HEAD), lambda i: (i, 0, 0, 0)),
                   pl.BlockSpec((nb, nc, DN_CONV_CH), lambda i: (i, 0, 0))],
        out_shape=[jax.ShapeDtypeStruct((db, DN_MIX), F32),
                   jax.ShapeDtypeStruct((db, DN_HEADS, DN_HEAD, DN_HEAD), F32),
                   jax.ShapeDtypeStruct((db, nc, DN_CONV_CH), F32)],
        compiler_params=_cparams("parallel"),
        name="dn_step",
    )(d_cols, conv_w, al, dtb, gn, state, conv_buf)


def _sb_step_kernel(pt_ref, q_ref, b_ref, *refs, n_pages):
    del pt_ref
    k_refs = refs[:n_pages]
    v_refs = refs[n_pages:2 * n_pages]
    o_ref = refs[2 * n_pages]
    width = q_ref.shape[2]
    head_mask = (_iota((SB_HEADS, width), 0) == _iota((SB_HEADS, width), 1) // SB_HEAD).astype(F32)
    qh = (q_ref[0] * SB_HEAD ** -0.5 * head_mask).astype(BF16)
    bias = b_ref[:, 0:1]
    pages = range(n_pages)
    page = k_refs[0].shape[3]
    upper = (_iota((page, page), 0) > _iota((page, page), 1)).astype(BF16)
    zs = [_dot(qh, k_refs[p][0, 0]) + bias for p in pages]
    stay = [jnp.minimum(-z, 0.0) - jnp.log(1.0 + jnp.exp(-jnp.abs(z))) for z in zs]
    parts = [_split2(x) for x in stay]
    later = [jnp.dot(hi, upper, preferred_element_type=F32) + jnp.dot(lo, upper, preferred_element_type=F32)
             for hi, lo in parts]
    sums = [jnp.sum(x, axis=-1, keepdims=True) for x in stay]
    runs = [None] * n_pages
    run = jnp.zeros((SB_HEADS, 1), F32)
    for p in reversed(pages):
        runs[p] = run
        run = run + sums[p]
    ws = [jnp.exp(zs[p] + stay[p] + (later[p] + runs[p])) for p in pages]
    acc = sum(_dot_nt(ws[p], v_refs[p][0, 0]) for p in pages)
    o_ref[0] = jnp.sum(acc * head_mask, axis=0, keepdims=True)


def sb_step(q, bias, cache_k, cache_v, page_table, *, layer):
    db, width = q.shape
    n_layers, n_phys, page = cache_k.shape[:3]
    n_pages = page_table.shape[1]
    bias_rows = jnp.broadcast_to(bias.reshape(SB_HEADS, 1), (SB_HEADS, LANES))

    def view(c):
        return jnp.transpose(c, (0, 1, 3, 4, 2)).reshape(n_layers, n_phys, width, page)

    def page_spec(p):
        return pl.BlockSpec((1, 1, width, page), lambda b, pt: (layer, pt[b, p], 0, 0))

    grid_spec = pltpu.PrefetchScalarGridSpec(
        num_scalar_prefetch=1,
        grid=(db,),
        in_specs=[pl.BlockSpec((1, 1, width), lambda b, pt: (b, 0, 0)),
                  pl.BlockSpec((SB_HEADS, LANES), lambda b, pt: (0, 0))]
                 + [page_spec(p) for p in range(n_pages)] * 2,
        out_specs=pl.BlockSpec((1, 1, width), lambda b, pt: (b, 0, 0)))
    out = pl.pallas_call(
        functools.partial(_sb_step_kernel, n_pages=n_pages),
        grid_spec=grid_spec,
        out_shape=jax.ShapeDtypeStruct((db, 1, width), F32),
        compiler_params=_cparams("parallel"),
        name="sb_step",
    )(page_table, q.reshape(db, 1, width), bias_rows,
      *([view(cache_k)] * n_pages), *([view(cache_v)] * n_pages))
    return out.reshape(db, width)


PROJ_PAD = 256


def _pad_cols(w, mult):
    n = w.shape[1]
    return jnp.pad(w, ((0, 0), (0, _round_up(n, mult) - n)))


def kernel(x_prompt, x_sample, mem_prompt, state_gla, state_rwkv, state_rwkv_shift, cache_sb_k, cache_sb_v, page_table, state_delta, state_delta_conv, cache_mem_k, cache_mem_v, norm_mix, norm_cross, norm_ffn, w_in_even, w_out_even, gla_w_a2, gla_b_a, gla_norm, rwkv_mu, rwkv_w0, rwkv_w2, rwkv_a0, rwkv_a2, rwkv_g2, rwkv_k_k, rwkv_k_a, rwkv_r_k, rwkv_ln_w, rwkv_ln_b, w_in_odd, w_out_odd, sb_q_norm, sb_k_norm, sb_bias, dn_conv, dn_a_log, dn_dt_bias, dn_norm, mem_norm, w_xq, w_xkv, w_xo, xq_norm, xk_norm, w_gu, w_down):
    b, t, d = x_prompt.shape
    db = x_sample.shape[0]
    depth = norm_mix.shape[0]
    mem_len = mem_prompt.shape[1]
    xh = d // X_HEADS
    gla_cols = 2 * GLA_HEADS * GLA_DK + 2 * GLA_HEADS * GLA_DV + GLA_LR

    xp = x_prompt.reshape(b * t, d)
    xs = x_sample.reshape(db, d)
    mem = mem_prompt.reshape(b * mem_len, d)

    p_gla, p_rwkv, p_shift, p_sbk, p_sbv, p_dn, p_conv, p_mk, p_mv = ([] for _ in range(9))
    s_gla, s_rwkv, s_shift, s_sbk, s_sbv, s_dn, s_conv = ([] for _ in range(7))

    for i in range(depth):
        j = i // 2
        if i % 2 == 0:
            w_in = w_in_even[j].astype(BF16)
            w_g = _pad_cols(w_in[:, :gla_cols], PROJ_PAD)
            w_r = w_in[:, gla_cols:]
            w_out = w_out_even[j].astype(BF16)
            wo_a, wo_b = w_out[:GLA_HEADS * GLA_DV], w_out[GLA_HEADS * GLA_DV:]
            rp = _rwkv_params(rwkv_mu[j], rwkv_w0[j], rwkv_w2[j], rwkv_a0[j], rwkv_a2[j], rwkv_g2[j],
                              rwkv_k_k[j], rwkv_k_a[j], rwkv_r_k[j], rwkv_ln_w[j], rwkv_ln_b[j])
            g_cols = norm_matmul(xp, norm_mix[i], w_g)
            r_cols = norm_matmul(xp, norm_mix[i], w_r)
            o_a, sg = gla_prompt(g_cols, gla_w_a2[j], gla_b_a[j], gla_norm[j], batch=b)
            o_b, sr, sh = rwkv_prompt(r_cols, rp, batch=b)
            xp = matmul_residual([(o_a, wo_a), (o_b, wo_b)], xp)
            p_gla.append(sg)
            p_rwkv.append(sr)
            p_shift.append(sh.reshape(b, -1))
            g_cols = norm_matmul(xs, norm_mix[i], w_g)
            r_cols = norm_matmul(xs, norm_mix[i], w_r)
            o_a, sg = gla_step(g_cols, gla_w_a2[j], gla_b_a[j], gla_norm[j], state_gla, layer=j)
            o_b, sr = rwkv_step(r_cols, rp, state_rwkv, state_rwkv_shift, layer=j)
            xs = matmul_residual([(o_a, wo_a), (o_b, wo_b)], xs)
            s_gla.append(sg)
            s_rwkv.append(sr)
            s_shift.append(r_cols)
        else:
            w_in = w_in_odd[j].astype(BF16)
            w_q, w_k, w_v = (w_in[:, n * SB_MIX:(n + 1) * SB_MIX] for n in range(3))
            w_d = _pad_cols(w_in[:, 3 * SB_MIX:], PROJ_PAD)
            w_out = w_out_odd[j].astype(BF16)
            wo_a, wo_b = w_out[:SB_MIX], w_out[SB_MIX:]
            qn = jnp.tile(sb_q_norm[j], SB_HEADS)
            kn = jnp.tile(sb_k_norm[j], SB_HEADS)
            dp = _dn_params(dn_conv[j], dn_a_log[j], dn_dt_bias[j], dn_norm[j])
            q = norm_matmul(xp, norm_mix[i], w_q, head_gain=qn, head_group=SB_HEAD)
            k = norm_matmul(xp, norm_mix[i], w_k, head_gain=kn, head_group=SB_HEAD)
            v = norm_matmul(xp, norm_mix[i], w_v)
            d_cols = norm_matmul(xp, norm_mix[i], w_d)
            o_a = sb_prompt(q, k, v, sb_bias[j], batch=b)
            o_b, sd, sc = dn_prompt(d_cols, dp, batch=b)
            xp = matmul_residual([(o_a, wo_a), (o_b, wo_b)], xp)
            p_sbk.append(k.reshape(b, t, SB_HEADS, SB_HEAD))
            p_sbv.append(v.reshape(b, t, SB_HEADS, SB_HEAD))
            p_dn.append(sd)
            p_conv.append(sc)
            q = norm_matmul(xs, norm_mix[i], w_q, head_gain=qn, head_group=SB_HEAD)
            k = norm_matmul(xs, norm_mix[i], w_k, head_gain=kn, head_group=SB_HEAD)
            v = norm_matmul(xs, norm_mix[i], w_v)
            d_cols = norm_matmul(xs, norm_mix[i], w_d)
            o_a = sb_step(q, sb_bias[j], cache_sb_k, cache_sb_v, page_table, layer=j)
            o_b, sd, sc = dn_step(d_cols, dp, state_delta, state_delta_conv, layer=j)
            xs = matmul_residual([(o_a, wo_a), (o_b, wo_b)], xs)
            s_sbk.append(k.reshape(db, 1, SB_HEADS, SB_HEAD))
            s_sbv.append(v.reshape(db, 1, SB_HEADS, SB_HEAD))
            s_dn.append(sd)
            s_conv.append(sc)

        w_kv = w_xkv[i].astype(BF16)
        wq = w_xq[i].astype(BF16)
        wo = w_xo[i].astype(BF16)
        mk = norm_matmul(mem, mem_norm[i], w_kv[:, :d], head_gain=jnp.tile(xk_norm[i], X_HEADS), head_group=xh)
        mv = norm_matmul(mem, mem_norm[i], w_kv[:, d:])
        p_mk.append(mk.reshape(b, mem_len, X_HEADS, xh))
        p_mv.append(mv.reshape(b, mem_len, X_HEADS, xh))
        xp = cross_attn_prompt(xp, norm_cross[i], wq, xq_norm[i], mk, mv, wo, batch=b)
        q = norm_matmul(xs, norm_cross[i], wq, head_gain=jnp.tile(xq_norm[i], X_HEADS), head_group=xh)
        xs = matmul_residual([(cross_attn_sample(q, cache_mem_k, cache_mem_v, layer=i), wo)], xs)

        wgu = w_gu[i].astype(BF16)
        wdn = w_down[i].astype(BF16)
        xp = ffn(xp, norm_ffn[i], wgu, wdn)
        xs = ffn(xs, norm_ffn[i], wgu, wdn)

    return (xp.reshape(b, t, d), xs.reshape(db, 1, d),
            jnp.stack(p_gla), jnp.stack(p_rwkv), jnp.stack(p_shift),
            jnp.stack(p_sbk), jnp.stack(p_sbv), jnp.stack(p_dn), jnp.stack(p_conv),
            jnp.stack(p_mk), jnp.stack(p_mv),
            jnp.stack(s_gla), jnp.stack(s_rwkv), jnp.stack(s_shift),
            jnp.stack(s_sbk), jnp.stack(s_sbv), jnp.stack(s_dn), jnp.stack(s_conv))
```

```python
import functools
import math

import jax
import jax.numpy as jnp
from jax import lax
from jax.experimental import pallas as pl
from jax.experimental.pallas import tpu as pltpu

F32 = jnp.float32
BF16 = jnp.bfloat16

LANES = 128
SUBLANES = 8
VMEM_LIMIT_BYTES = 56 * 1024 * 1024
MAX_TN = 2304

EPS = 1e-6
PAGE_SIZE = 128

GLA_HEADS, GLA_DK, GLA_DV, GLA_LR = 4, 64, 128, 16
GLA_GATE_NORM = 16.0
GLA_CHUNK = 64
GLA_SUB = 16
RWKV_HEADS, RWKV_HEAD = 8, 64
RWKV_W_LR, RWKV_A_LR, RWKV_G_LR = 64, 64, 128
RWKV_LN_EPS = 64e-5
RWKV_CHUNK = 64
SB_HEADS, SB_HEAD = 8, 64
DN_HEADS, DN_HEAD, DN_CONV = 4, 128, 4
DN_CHUNK = 64
X_HEADS = 4


def _cparams(*sem):
    return pltpu.CompilerParams(dimension_semantics=sem, vmem_limit_bytes=VMEM_LIMIT_BYTES)


def _round_up(n, m):
    return -(-n // m) * m


def _pick_tile(n, candidates):
    for c in candidates:
        if n % c == 0:
            return c
    return n


def _dot(a, b):
    return jnp.dot(a.astype(BF16), b.astype(BF16), preferred_element_type=F32)


def _dot_nt(a, b):
    return lax.dot_general(a.astype(BF16), b.astype(BF16), (((1,), (1,)), ((), ())),
                           preferred_element_type=F32)


def _dot_tn(a, b):
    return lax.dot_general(a.astype(BF16), b.astype(BF16), (((0,), (0,)), ((), ())),
                           preferred_element_type=F32)


def _split3(a):
    hi = a.astype(BF16)
    r1 = a - hi.astype(F32)
    mid = r1.astype(BF16)
    lo = (r1 - mid.astype(F32)).astype(BF16)
    return hi, mid, lo


def _dot_x3(a, b):
    b = b.astype(BF16)
    hi, mid, lo = _split3(a)
    return (jnp.dot(hi, b, preferred_element_type=F32) + jnp.dot(mid, b, preferred_element_type=F32)
            + jnp.dot(lo, b, preferred_element_type=F32))


def _dot_3x(a, b):
    a = a.astype(BF16)
    hi, mid, lo = _split3(b)
    return (jnp.dot(a, hi, preferred_element_type=F32) + jnp.dot(a, mid, preferred_element_type=F32)
            + jnp.dot(a, lo, preferred_element_type=F32))


def _split2(a):
    hi = a.astype(BF16)
    return hi, (a - hi.astype(F32)).astype(BF16)


def _dot_hp(a, b):
    a_hi, a_lo = _split2(a)
    b_hi, b_lo = _split2(b)
    return (jnp.dot(a_hi, b_hi, preferred_element_type=F32)
            + (jnp.dot(a_hi, b_lo, preferred_element_type=F32) + jnp.dot(a_lo, b_hi, preferred_element_type=F32)))


def _iota(shape, dim):
    return lax.broadcasted_iota(jnp.int32, shape, dim)


def _group_ones(n, group):
    return (_iota((n, n), 0) // group == _iota((n, n), 1) // group).astype(F32)


def _group_sum(x, group):
    width = x.shape[-1]
    slab = min(width, max(group, LANES))
    ones = _group_ones(slab, group)
    parts = [_dot_x3(x[:, s:s + slab], ones) for s in range(0, width, slab)]
    return parts[0] if len(parts) == 1 else jnp.concatenate(parts, axis=-1)


def _rms(x, gain):
    return x * lax.rsqrt(jnp.mean(x * x, axis=-1, keepdims=True) + EPS) * gain


def _group_rms(x, gain_row, group):
    ms = _group_sum(x * x, group) * (1.0 / group)
    return x * lax.rsqrt(ms + EPS) * gain_row


def _silu(x):
    return x * jax.nn.sigmoid(x)


def _softplus(x):
    return jnp.maximum(x, 0.0) + jnp.log1p(jnp.exp(-jnp.abs(x)))


def _log_sigmoid(x):
    return -_softplus(-x)


def _norm_matmul_kernel(x_ref, g_ref, w_ref, *rest, head_group):
    if head_group:
        hg_ref, o_ref, h_ref = rest
    else:
        o_ref, h_ref = rest

    @pl.when(pl.program_id(1) == 0)
    def _():
        h_ref[...] = _rms(x_ref[...], g_ref[...]).astype(BF16)

    y = jnp.dot(h_ref[...], w_ref[...], preferred_element_type=F32)
    if head_group:
        y = _group_rms(y, hg_ref[...], head_group)
    o_ref[...] = y.astype(o_ref.dtype)


def norm_matmul(x, gain, w, *, head_gain=None, head_group=0, out_dtype=F32):
    m, k = x.shape
    n = w.shape[1]
    tm = _pick_tile(m, (1024, 512, 256, 128))
    tn = n if n <= MAX_TN else _pick_tile(n, (1024, 768, 512, 384, 256, 128))
    in_specs = [pl.BlockSpec((tm, k), lambda i, j: (i, 0)),
                pl.BlockSpec((1, k), lambda i, j: (0, 0)),
                pl.BlockSpec((k, tn), lambda i, j: (0, j))]
    args = [x, gain.reshape(1, k), w]
    if head_group:
        in_specs.append(pl.BlockSpec((1, tn), lambda i, j: (0, j)))
        args.append(head_gain.reshape(1, n))
    return pl.pallas_call(
        functools.partial(_norm_matmul_kernel, head_group=head_group),
        grid=(m // tm, n // tn),
        in_specs=in_specs,
        out_specs=pl.BlockSpec((tm, tn), lambda i, j: (i, j)),
        out_shape=jax.ShapeDtypeStruct((m, n), out_dtype),
        scratch_shapes=[pltpu.VMEM((tm, k), BF16)],
        compiler_params=_cparams("parallel", "arbitrary"),
        name="norm_matmul",
    )(*args)


def _matmul_res_kernel(*refs, n_pairs):
    res_ref = refs[2 * n_pairs]
    o_ref = refs[2 * n_pairs + 1]
    acc = res_ref[...]
    for p in range(n_pairs):
        acc = acc + jnp.dot(refs[2 * p][...].astype(BF16), refs[2 * p + 1][...],
                            preferred_element_type=F32)
    o_ref[...] = acc


def matmul_residual(pairs, res):
    m, n = res.shape
    tm = _pick_tile(m, (1024, 512, 256, 128))
    tn = n if n <= MAX_TN else _pick_tile(n, (1024, 512, 256, 128))
    in_specs, args = [], []
    for a, w in pairs:
        kk = a.shape[1]
        in_specs.append(pl.BlockSpec((tm, kk), lambda i, j: (i, 0)))
        in_specs.append(pl.BlockSpec((kk, tn), lambda i, j: (0, j)))
        args += [a, w]
    in_specs.append(pl.BlockSpec((tm, tn), lambda i, j: (i, j)))
    args.append(res)
    return pl.pallas_call(
        functools.partial(_matmul_res_kernel, n_pairs=len(pairs)),
        grid=(m // tm, n // tn),
        in_specs=in_specs,
        out_specs=pl.BlockSpec((tm, tn), lambda i, j: (i, j)),
        out_shape=jax.ShapeDtypeStruct((m, n), F32),
        compiler_params=_cparams("parallel", "parallel"),
        name="matmul_residual",
    )(*args)


FFN_CHUNK = 256


def _ffn_kernel(x_ref, g_ref, wgu_ref, wd_ref, o_ref):
    x = x_ref[...]
    ff = wd_ref.shape[0]
    h = _rms(x, g_ref[...]).astype(BF16)
    acc = x
    for f in range(0, ff, FFN_CHUNK):
        gate = jnp.dot(h, wgu_ref[:, f:f + FFN_CHUNK], preferred_element_type=F32)
        up = jnp.dot(h, wgu_ref[:, ff + f:ff + f + FFN_CHUNK], preferred_element_type=F32)
        act = (_silu(gate) * up).astype(BF16)
        acc = acc + jnp.dot(act, wd_ref[f:f + FFN_CHUNK, :], preferred_element_type=F32)
    o_ref[...] = acc


def ffn(x, gain, w_gu, w_down):
    m, d = x.shape
    ff = w_down.shape[0]
    assert ff % FFN_CHUNK == 0
    tm = _pick_tile(m, (512, 256, 128))
    resident = pl.Buffered(1)
    return pl.pallas_call(
        _ffn_kernel,
        grid=(m // tm,),
        in_specs=[pl.BlockSpec((tm, d), lambda i: (i, 0)),
                  pl.BlockSpec((1, d), lambda i: (0, 0)),
                  pl.BlockSpec((d, 2 * ff), lambda i: (0, 0), pipeline_mode=resident),
                  pl.BlockSpec((ff, d), lambda i: (0, 0), pipeline_mode=resident)],
        out_specs=pl.BlockSpec((tm, d), lambda i: (i, 0)),
        out_shape=jax.ShapeDtypeStruct((m, d), F32),
        compiler_params=_cparams("parallel"),
        name="ffn",
    )(x, gain.reshape(1, d), w_gu, w_down)


def _softmax_rows(s):
    p = jnp.exp(s - jnp.max(s, axis=-1, keepdims=True))
    return p / jnp.sum(p, axis=-1, keepdims=True)


def _cross_prompt_kernel(x_ref, g_ref, wq_ref, qg_ref, mk_ref, mv_ref, wo_ref, o_ref, *, heads):
    x = x_ref[...]
    d = x.shape[1]
    dh = d // heads
    h = _rms(x, g_ref[...]).astype(BF16)
    q = jnp.dot(h, wq_ref[...], preferred_element_type=F32)
    mk = mk_ref[...].astype(BF16)
    mv = mv_ref[...].astype(BF16)
    outs = []
    for hh in range(heads):
        sl = slice(hh * dh, (hh + 1) * dh)
        qh = _rms(q[:, sl], qg_ref[...])
        s = _dot_nt(qh, mk[:, sl]) * dh ** -0.5
        outs.append(_dot(_softmax_rows(s), mv[:, sl]))
    o = jnp.concatenate(outs, axis=-1)
    o_ref[...] = x + _dot(o, wo_ref[...])


def cross_attn_prompt(x, gain, w_xq, xq_norm, mk, mv, w_xo, *, batch):
    m, d = x.shape
    t = m // batch
    mem_len = mk.shape[0] // batch
    dh = d // X_HEADS
    tm = _pick_tile(t, (512, 256, 128))
    nt = t // tm
    return pl.pallas_call(
        functools.partial(_cross_prompt_kernel, heads=X_HEADS),
        grid=(batch, nt),
        in_specs=[pl.BlockSpec((tm, d), lambda b, i: (b * nt + i, 0)),
                  pl.BlockSpec((1, d), lambda b, i: (0, 0)),
                  pl.BlockSpec((d, d), lambda b, i: (0, 0)),
                  pl.BlockSpec((1, dh), lambda b, i: (0, 0)),
                  pl.BlockSpec((mem_len, d), lambda b, i: (b, 0)),
                  pl.BlockSpec((mem_len, d), lambda b, i: (b, 0)),
                  pl.BlockSpec((d, d), lambda b, i: (0, 0))],
        out_specs=pl.BlockSpec((tm, d), lambda b, i: (b * nt + i, 0)),
        out_shape=jax.ShapeDtypeStruct((m, d), F32),
        compiler_params=_cparams("parallel", "parallel"),
        name="cross_attn_prompt",
    )(x, gain.reshape(1, d), w_xq, xq_norm.reshape(1, dh), mk, mv, w_xo)


def _cross_sample_kernel(q_ref, mk_ref, mv_ref, o_ref, *, heads, mem_len):
    nb = q_ref.shape[0]
    d = q_ref.shape[2]
    dh = d // heads
    halves = dh // LANES
    assert heads * halves == SUBLANES
    row = _iota((SUBLANES, LANES), 0)
    for b in range(nb):
        q = q_ref[b] * dh ** -0.5
        s = jnp.zeros((SUBLANES, mem_len), F32)
        for r in range(SUBLANES):
            c, h = divmod(r, heads)
            chunk = q[:, h * dh + c * LANES:h * dh + (c + 1) * LANES]
            q_rows = jnp.where(row % heads == h, chunk, 0.0)
            s = s + _dot_nt(q_rows, mk_ref[0, b, pl.ds(r, mem_len, stride=SUBLANES), :])
        a = _softmax_rows(s)
        out = jnp.zeros((SUBLANES, LANES), F32)
        for r in range(SUBLANES):
            o_r = _dot(a, mv_ref[0, b, pl.ds(r, mem_len, stride=SUBLANES), :])
            out = out + jnp.where(row == r, o_r, 0.0)
        o_ref[b] = out


def cross_attn_sample(q, cache_k, cache_v, *, layer):
    db, d = q.shape
    depth, _, mem_len, heads, dh = cache_k.shape
    halves = dh // LANES

    def view(c):
        c = c.reshape(depth, db, mem_len, heads, halves, LANES)
        return c.transpose(0, 1, 2, 4, 3, 5).reshape(depth, db, mem_len * halves * heads, LANES)

    nb = _pick_tile(db, (4, 2, 1))
    rows = mem_len * halves * heads
    out = pl.pallas_call(
        functools.partial(_cross_sample_kernel, heads=heads, mem_len=mem_len),
        grid=(db // nb,),
        in_specs=[pl.BlockSpec((nb, 1, d), lambda i: (i, 0, 0)),
                  pl.BlockSpec((1, nb, rows, LANES), lambda i: (layer, i, 0, 0)),
                  pl.BlockSpec((1, nb, rows, LANES), lambda i: (layer, i, 0, 0))],
        out_specs=pl.BlockSpec((nb, halves * heads, LANES), lambda i: (i, 0, 0)),
        out_shape=jax.ShapeDtypeStruct((db, halves * heads, LANES), F32),
        compiler_params=_cparams("parallel"),
        name="cross_attn_sample",
    )(q.reshape(db, 1, d), view(cache_k), view(cache_v))
    return out.reshape(db, halves, heads, LANES).transpose(0, 2, 1, 3).reshape(db, d)


def _row_to_col(row):
    n = row.shape[1]
    eye = (_iota((n, n), 0) == _iota((n, n), 1)).astype(F32)
    return jnp.sum(eye * row, axis=-1, keepdims=True)


def _tri(n, strict=False):
    r, c = _iota((n, n), 0), _iota((n, n), 1)
    return ((r > c) if strict else (r >= c)).astype(F32)


GLA_PAIR = LANES // GLA_DK


def _gla_chunk_pairs(qs, ks, bs, vs, states):
    c = qs[0].shape[0]
    pairs = range(len(qs))
    sub = GLA_SUB
    nsub = c // sub
    lane = _iota((1, LANES), 1)
    masks = [(lane // GLA_DK == r).astype(F32) for r in range(GLA_PAIR)]
    ones = _group_ones(LANES, GLA_DK).astype(BF16)
    t_idx = _iota((c, LANES), 0)
    s_idx = _iota((c, LANES), 1) % c
    qe = [qs[p] * jnp.exp(bs[p]) for p in pairs]
    outs = [_dot(qe[p] * masks[r], states[p]) for p in pairs for r in range(GLA_PAIR)]
    rows = [[jnp.zeros((sub, LANES), F32)] for _ in pairs]
    for j in range(1, nsub):
        blk = slice(sub * j, sub * (j + 1))
        keep = _iota((sub, LANES), 1) % c < sub * j
        for p in pairs:
            ref = bs[p][sub * j - 1:sub * j, :]
            qj = qs[p][blk, :] * jnp.exp(bs[p][blk, :] - ref)
            kj = ks[p] * jnp.exp(jnp.minimum(ref - bs[p], 0.0))
            pj = jnp.concatenate([_dot_nt(qj * masks[r], kj) for r in range(GLA_PAIR)], axis=-1)
            rows[p].append(jnp.where(keep, pj, 0.0))
    pm = [jnp.concatenate(rows[p], axis=0) for p in pairs]
    for i in range(sub):
        hit = (s_idx == (t_idx // sub) * sub + i) & (t_idx % sub >= i)
        for p in pairs:
            ksel = jnp.concatenate(
                [jnp.broadcast_to(ks[p][sub * j + i:sub * j + i + 1, :], (sub, LANES)) for j in range(nsub)], axis=0)
            bsel = jnp.concatenate(
                [jnp.broadcast_to(bs[p][sub * j + i:sub * j + i + 1, :], (sub, LANES)) for j in range(nsub)], axis=0)
            hi, lo = _split2(qs[p] * ksel * jnp.exp(jnp.minimum(bs[p] - bsel, 0.0)))
            col = jnp.dot(hi, ones, preferred_element_type=F32) + jnp.dot(lo, ones, preferred_element_type=F32)
            pm[p] = jnp.where(hit, col, pm[p])
    outs = [outs[p * GLA_PAIR + r]
            + _dot(pm[p] * masks[r], jnp.concatenate([vs[p * GLA_PAIR + r]] * GLA_PAIR, axis=0))
            for p in pairs for r in range(GLA_PAIR)]
    lasts = [b[c - 1:c, :] for b in bs]
    kd = [ks[p] * jnp.exp(lasts[p] - bs[p]) for p in pairs]
    new = [_row_to_col(jnp.exp(lasts[p])) * states[p]
           + sum(_dot_tn(kd[p] * masks[r], vs[p * GLA_PAIR + r]) for r in range(GLA_PAIR)) for p in pairs]
    return outs, new


def _gla_prompt_kernel(g_ref, wa_ref, ba_ref, gn_ref, o_ref, sout_ref, s_ref):
    tb = g_ref.shape[0]
    c = GLA_CHUNK
    nk = GLA_HEADS * GLA_DK
    half = GLA_HEADS * GLA_DV
    lr_off = 2 * nk + 2 * half
    n_pairs = GLA_HEADS // GLA_PAIR
    assert GLA_PAIR * c == LANES

    @pl.when(pl.program_id(1) == 0)
    def _():
        s_ref[...] = jnp.zeros_like(s_ref)

    def chunk(ci, carry):
        rows = pl.ds(pl.multiple_of(ci * c, c), c)
        lr = g_ref[rows, lr_off:lr_off + LANES]
        lf = _log_sigmoid(_dot(lr, wa_ref[...]) + ba_ref[...]) * (1.0 / GLA_GATE_NORM)
        bcum = _dot_3x(_tri(c), lf)
        outs, new = _gla_chunk_pairs(
            [g_ref[rows, p * LANES:(p + 1) * LANES] * GLA_DK ** -0.5 for p in range(n_pairs)],
            [g_ref[rows, nk + p * LANES:nk + (p + 1) * LANES] for p in range(n_pairs)],
            [bcum[:, p * LANES:(p + 1) * LANES] for p in range(n_pairs)],
            [g_ref[rows, 2 * nk + h * GLA_DV:2 * nk + (h + 1) * GLA_DV] for h in range(GLA_HEADS)],
            [s_ref[p] for p in range(n_pairs)])
        for p in range(n_pairs):
            s_ref[p] = new[p]
        for h in range(GLA_HEADS):
            gate = g_ref[rows, 2 * nk + half + h * GLA_DV:2 * nk + half + (h + 1) * GLA_DV]
            o_ref[rows, h * GLA_DV:(h + 1) * GLA_DV] = (_rms(outs[h], gn_ref[...]) * _silu(gate)).astype(o_ref.dtype)
        return carry

    lax.fori_loop(0, tb // c, chunk, 0)
    sout_ref[0] = s_ref[...]


def gla_prompt(g_cols, w_a2, b_a, gnorm, *, batch):
    m, width = g_cols.shape
    t = m // batch
    tb = _pick_tile(t, (256, 128, 64))
    nt = t // tb
    nk = GLA_HEADS * GLA_DK
    half = GLA_HEADS * GLA_DV
    wa = jnp.zeros((LANES, nk), F32).at[:GLA_LR].set(w_a2)
    n_pairs = GLA_HEADS // GLA_PAIR
    o, s_pairs = pl.pallas_call(
        _gla_prompt_kernel,
        grid=(batch, nt),
        in_specs=[pl.BlockSpec((tb, width), lambda b, i: (b * nt + i, 0)),
                  pl.BlockSpec((LANES, nk), lambda b, i: (0, 0)),
                  pl.BlockSpec((1, nk), lambda b, i: (0, 0)),
                  pl.BlockSpec((1, GLA_DV), lambda b, i: (0, 0))],
        out_specs=[pl.BlockSpec((tb, half), lambda b, i: (b * nt + i, 0)),
                   pl.BlockSpec((1, n_pairs, LANES, GLA_DV), lambda b, i: (b, 0, 0, 0))],
        out_shape=[jax.ShapeDtypeStruct((m, half), BF16),
                   jax.ShapeDtypeStruct((batch, n_pairs, LANES, GLA_DV), F32)],
        scratch_shapes=[pltpu.VMEM((n_pairs, LANES, GLA_DV), F32)],
        compiler_params=_cparams("parallel", "arbitrary"),
        name="gla_prompt",
    )(g_cols, wa, b_a.reshape(1, nk), gnorm.reshape(1, GLA_DV))
    return o, s_pairs.reshape(batch, GLA_HEADS, GLA_DK, GLA_DV)


RWKV_MIX = RWKV_HEADS * RWKV_HEAD
RWKV_COLS = 3 * RWKV_MIX + RWKV_W_LR + RWKV_A_LR + RWKV_G_LR
RWKV_PARAM_ROWS = 8


def _unit_lower_inverses(mats):
    c = mats[0].shape[0]
    eye = (_iota((c, c), 0) == _iota((c, c), 1)).astype(F32)
    prods = [eye + m for m in mats]
    powers = list(mats)
    span = 2
    while span < c:
        powers = [_dot_hp(m, m) for m in powers]
        prods = [p + _dot_hp(p, m) for p, m in zip(prods, powers)]
        span *= 2
    return prods


def _rwkv_prep(r, prev, mu, lora_w, lora_a, g2, tab):
    m = RWKV_MIX
    xr = r + (prev - r) * mu
    rr, rk, rv = xr[:, :m], xr[:, m:2 * m], xr[:, 2 * m:3 * m]
    wa_in = xr[:, 3 * m:3 * m + LANES]
    wa_in = jnp.where(_iota(wa_in.shape, 1) < RWKV_W_LR, jnp.tanh(wa_in), wa_in)
    rg = xr[:, 3 * m + LANES:3 * m + 2 * LANES]
    w0, a0, k_k, k_a = tab[0:1], tab[1:2], tab[2:3], tab[3:4]
    w_log = -_softplus(-(w0 + _dot(wa_in, lora_w))) - 0.5
    log_decay = -jnp.exp(w_log)
    a = jax.nn.sigmoid(a0 + _dot(wa_in, lora_a))
    g = _dot(jax.nn.sigmoid(rg), g2)
    kx = rk * k_k
    kk = kx * lax.rsqrt(_group_sum(kx * kx, RWKV_HEAD) + EPS)
    k2 = rk * (1.0 + (a - 1.0) * k_a)
    return rr, k2, rv, kk, a, log_decay, g


def _rwkv_finish(y, rr, k2, rv, g, tab):
    r_k, ln_w, ln_b = tab[4:5], tab[5:6], tab[6:7]
    inv = 1.0 / RWKV_HEAD
    mean = _group_sum(y, RWKV_HEAD) * inv
    yc = y - mean
    var = _group_sum(yc * yc, RWKV_HEAD) * inv
    yn = yc * lax.rsqrt(var + RWKV_LN_EPS) * ln_w + ln_b
    bonus = _group_sum(rr * k2 * r_k, RWKV_HEAD) * rv
    return (yn + bonus) * g


def _rwkv_chunk_heads(at, rt, bt, kt, bh, kh, v, decay_c, states):
    c = at.shape[0]
    hd = RWKV_HEAD
    heads = range(len(states))
    sls = [slice(h * hd, (h + 1) * hd) for h in heads]
    strict = _tri(c, strict=True)
    incl = _tri(c)
    mms = [_dot_nt(jnp.concatenate([at[:, sl], rt[:, sl]], axis=0),
                   jnp.concatenate([bt[:, sl], kt[:, sl]], axis=0)) for sl in sls]
    t_invs = _unit_lower_inverses([mm[:c, :c] * strict for mm in mms])
    xs = [_dot_nt(at[:, sls[h]], states[h]) + _dot(mms[h][:c, c:] * strict, v[:, sls[h]]) for h in heads]
    us = [_dot(t_invs[h], xs[h]) for h in heads]
    ys = [_dot_nt(rt[:, sls[h]], states[h]) + _dot(mms[h][c:, :c] * incl, us[h])
          + _dot(mms[h][c:, c:] * incl, v[:, sls[h]]) for h in heads]
    new = [states[h] * decay_c[:, sls[h]] + _dot_tn(us[h], bh[:, sls[h]]) + _dot_tn(v[:, sls[h]], kh[:, sls[h]])
           for h in heads]
    return ys, new


def _rwkv_prompt_kernel(r_ref, mu_ref, lw_ref, la_ref, g2_ref, tab_ref, o_ref, sout_ref, shout_ref,
                        s_ref, shift_ref, y_ref, f_ref):
    tb = r_ref.shape[0]
    c = RWKV_CHUNK
    hd = RWKV_HEAD

    @pl.when(pl.program_id(1) == 0)
    def _():
        s_ref[...] = jnp.zeros_like(s_ref)
        shift_ref[...] = jnp.zeros_like(shift_ref)

    r = r_ref[...]
    prev = jnp.where(_iota(r.shape, 0) == 0, shift_ref[...], pltpu.roll(r, 1, axis=0))
    shift_ref[...] = r[tb - 1:tb, :]
    tab = tab_ref[...]
    rr, k2, rv, kk, a, ld, g = _rwkv_prep(r, prev, mu_ref[...], lw_ref[...], la_ref[...], g2_ref[...], tab)
    f_ref[0] = rr
    f_ref[1] = k2
    f_ref[2] = rv
    f_ref[3] = kk
    f_ref[4] = a
    f_ref[5] = ld

    def chunk(ci, carry):
        rows = pl.ds(pl.multiple_of(ci * c, c), c)
        rr_c, k2_c, rv_c, kk_c, a_c, ld_c = (f_ref[i, rows, :] for i in range(6))
        gcum = _dot_3x(_tri(c), ld_c)
        g_end = gcum[c - 1:c, :]
        e_neg = jnp.exp(-gcum)
        e_end = jnp.exp(g_end - gcum)
        beta = kk_c * a_c
        at = -kk_c * jnp.exp(gcum - ld_c)
        rt = rr_c * jnp.exp(gcum)
        bt = beta * e_neg
        kt = k2_c * e_neg
        bh = beta * e_end
        kh = k2_c * e_end
        decay_c = jnp.exp(g_end)
        ys, new = _rwkv_chunk_heads(at, rt, bt, kt, bh, kh, rv_c, decay_c,
                                    [s_ref[h] for h in range(RWKV_HEADS)])
        for h in range(RWKV_HEADS):
            s_ref[h] = new[h]
            y_ref[rows, h * hd:(h + 1) * hd] = ys[h]
        return carry

    lax.fori_loop(0, tb // c, chunk, 0)
    o_ref[...] = _rwkv_finish(y_ref[...], rr, k2, rv, g, tab).astype(o_ref.dtype)
    sout_ref[0] = s_ref[...]
    shout_ref[0] = shift_ref[...]


def _rwkv_params(mu, w0, w2, a0, a2, g2, k_k, k_a, r_k, ln_w, ln_b):
    m = RWKV_MIX
    lora_w = jnp.zeros((LANES, m), F32).at[:RWKV_W_LR].set(w2)
    lora_a = jnp.zeros((LANES, m), F32).at[RWKV_W_LR:RWKV_W_LR + RWKV_A_LR].set(a2)
    tab = jnp.stack([w0, a0, k_k, k_a, r_k, ln_w, ln_b, jnp.zeros_like(w0)])
    return mu.reshape(1, RWKV_COLS), lora_w, lora_a, g2, tab


def rwkv_prompt(r_cols, params, *, batch):
    m, width = r_cols.shape
    t = m // batch
    tb = _pick_tile(t, (256, 128, 64))
    nt = t // tb
    mix = RWKV_MIX
    mu, lora_w, lora_a, g2, tab = params
    const = lambda b, i: (0, 0)
    return pl.pallas_call(
        _rwkv_prompt_kernel,
        grid=(batch, nt),
        in_specs=[pl.BlockSpec((tb, width), lambda b, i: (b * nt + i, 0)),
                  pl.BlockSpec((1, width), const),
                  pl.BlockSpec((LANES, mix), const),
                  pl.BlockSpec((LANES, mix), const),
                  pl.BlockSpec((RWKV_G_LR, mix), const),
                  pl.BlockSpec((RWKV_PARAM_ROWS, mix), const)],
        out_specs=[pl.BlockSpec((tb, mix), lambda b, i: (b * nt + i, 0)),
                   pl.BlockSpec((1, RWKV_HEADS, RWKV_HEAD, RWKV_HEAD), lambda b, i: (b, 0, 0, 0)),
                   pl.BlockSpec((1, 1, width), lambda b, i: (b, 0, 0))],
        out_shape=[jax.ShapeDtypeStruct((m, mix), BF16),
                   jax.ShapeDtypeStruct((batch, RWKV_HEADS, RWKV_HEAD, RWKV_HEAD), F32),
                   jax.ShapeDtypeStruct((batch, 1, width), F32)],
        scratch_shapes=[pltpu.VMEM((RWKV_HEADS, RWKV_HEAD, RWKV_HEAD), F32),
                        pltpu.VMEM((1, width), F32),
                        pltpu.VMEM((tb, mix), F32),
                        pltpu.VMEM((6, tb, mix), F32)],
        compiler_params=_cparams("parallel", "arbitrary"),
        name="rwkv_prompt",
    )(r_cols, mu, lora_w, lora_a, g2, tab)


DN_MIX = DN_HEADS * DN_HEAD
DN_CONV_CH = 3 * DN_MIX
DN_GATE_OFF = DN_CONV_CH + DN_MIX


def _col_to_row(col):
    n = col.shape[0]
    eye = (_iota((n, n), 0) == _iota((n, n), 1)).astype(F32)
    return jnp.sum(eye * col, axis=0, keepdims=True)


def _dn_gates(gb_tile, a_log_row, dt_bias_row):
    gdec = -jnp.exp(a_log_row) * _softplus(gb_tile + dt_bias_row)
    beta = jax.nn.sigmoid(gb_tile)
    return gdec, beta


def _l2n(x):
    return x * lax.rsqrt(jnp.sum(x * x, axis=-1, keepdims=True) + EPS)


def _dn_chunk_setup(qs, ks, vs, g_cols, beta_cols):
    c = qs[0].shape[0]
    pairs = range(len(qs))
    incl = _tri(c)
    strict = _tri(c, strict=True)
    dmats = [jnp.exp(jnp.minimum(g - _col_to_row(g), 0.0)) * incl for g in g_cols]
    kks = [_dot_nt(k, k) for k in ks]
    t_invs = _unit_lower_inverses([-(strict * beta_cols[n] * kks[n] * dmats[n]) for n in pairs])
    tvs = [_dot(t_invs[n], beta_cols[n] * vs[n]) for n in pairs]
    tks = [_dot(t_invs[n], (beta_cols[n] * jnp.exp(g_cols[n])) * ks[n]) for n in pairs]
    qks = [_dot_nt(qs[n], ks[n]) * dmats[n] for n in pairs]
    lasts = [g[c - 1:c, :] for g in g_cols]
    return [(tvs[n], tks[n], qks[n], qs[n] * jnp.exp(g_cols[n]), ks[n] * jnp.exp(lasts[n] - g_cols[n]),
             jnp.exp(lasts[n])) for n in pairs]


def _dn_chunk_scan(parts, states):
    heads = range(len(states))
    us = [parts[h][0] - _dot(parts[h][1], states[h]) for h in heads]
    outs = [_dot(parts[h][3], states[h]) + _dot(parts[h][2], us[h]) for h in heads]
    new = [parts[h][5] * states[h] + _dot_tn(parts[h][4], us[h]) for h in heads]
    return outs, new


def _shift_rows(x, k, before):
    out = pltpu.roll(x, k, axis=0)
    row = _iota(x.shape, 0)
    nb = DN_CONV - 1
    for r in range(k):
        out = jnp.where(row == r, before[nb - k + r:nb - k + r + 1, :], out)
    return out


def _dn_prompt_kernel(c_ref, cw_ref, al_ref, dtb_ref, gn_ref, o_ref, sout_ref, cout_ref,
                      s_ref, buf_ref, x_ref, g_ref, p_ref, qk_ref, dl_ref):
    tb = c_ref.shape[0]
    c = DN_CHUNK
    hd = DN_HEAD
    nb = DN_CONV - 1

    @pl.when(pl.program_id(1) == 0)
    def _():
        s_ref[...] = jnp.zeros_like(s_ref)
        buf_ref[...] = jnp.zeros_like(buf_ref)

    x = c_ref[:, :DN_CONV_CH]
    before = buf_ref[...]
    cw = cw_ref[...]
    y = x * cw[nb:nb + 1, :]
    for k in range(1, DN_CONV):
        y = y + _shift_rows(x, k, before) * cw[nb - k:nb - k + 1, :]
    buf_ref[...] = x[tb - nb:tb, :]
    xc = _silu(y)
    for h in range(DN_HEADS):
        x_ref[:, h * hd:(h + 1) * hd] = _l2n(xc[:, h * hd:(h + 1) * hd]) * hd ** -0.5
        x_ref[:, DN_MIX + h * hd:DN_MIX + (h + 1) * hd] = _l2n(xc[:, DN_MIX + h * hd:DN_MIX + (h + 1) * hd])
    x_ref[:, 2 * DN_MIX:] = xc[:, 2 * DN_MIX:]
    gdec, beta = _dn_gates(c_ref[:, DN_GATE_OFF:DN_GATE_OFF + LANES], al_ref[...], dtb_ref[...])
    g_ref[0] = gdec
    g_ref[1] = beta

    heads = range(DN_HEADS)
    n_chunks = tb // c
    group = 2 if n_chunks % 2 == 0 else 1

    def setup(gi, carry):
        qs, ks, vs, gs, bs = [], [], [], [], []
        for j in range(group):
            rows = pl.ds(pl.multiple_of((gi * group + j) * c, c), c)
            gcum = _dot_3x(_tri(c), g_ref[0, rows, :])
            beta_c = g_ref[1, rows, :]
            qs += [x_ref[rows, h * hd:(h + 1) * hd] for h in heads]
            ks += [x_ref[rows, DN_MIX + h * hd:DN_MIX + (h + 1) * hd] for h in heads]
            vs += [x_ref[rows, 2 * DN_MIX + h * hd:2 * DN_MIX + (h + 1) * hd] for h in heads]
            gs += [gcum[:, h:h + 1] for h in heads]
            bs += [beta_c[:, DN_HEADS + h:DN_HEADS + h + 1] for h in heads]
        parts = _dn_chunk_setup(qs, ks, vs, gs, bs)
        for j in range(group):
            for h in heads:
                tv, tk, qk, qg, kd, dl = parts[j * DN_HEADS + h]
                ci = gi * group + j
                p_ref[0, ci, h] = tv
                p_ref[1, ci, h] = tk
                p_ref[2, ci, h] = qg
                p_ref[3, ci, h] = kd
                qk_ref[ci, h] = qk
                dl_ref[ci, h] = jnp.broadcast_to(dl, (1, hd))
        return carry

    lax.fori_loop(0, n_chunks // group, setup, 0)

    def scan(ci, carry):
        rows = pl.ds(pl.multiple_of(ci * c, c), c)
        parts = [(p_ref[0, ci, h], p_ref[1, ci, h], qk_ref[ci, h], p_ref[2, ci, h], p_ref[3, ci, h], dl_ref[ci, h])
                 for h in heads]
        outs, new = _dn_chunk_scan(parts, [s_ref[h] for h in heads])
        for h in heads:
            s_ref[h] = new[h]
            z = c_ref[rows, DN_CONV_CH + h * hd:DN_CONV_CH + (h + 1) * hd]
            o_ref[rows, h * hd:(h + 1) * hd] = (_rms(outs[h], gn_ref[...]) * _silu(z)).astype(o_ref.dtype)
        return carry

    lax.fori_loop(0, n_chunks, scan, 0)
    sout_ref[0] = s_ref[...]
    cout_ref[0] = buf_ref[...]


def _dn_params(conv_w, a_log, dt_bias, gnorm):
    al = jnp.zeros((1, LANES), F32).at[0, :DN_HEADS].set(a_log)
    dtb = jnp.zeros((1, LANES), F32).at[0, :DN_HEADS].set(dt_bias)
    return conv_w, al, dtb, gnorm.reshape(1, DN_HEAD)


def dn_prompt(d_cols, params, *, batch):
    m, width = d_cols.shape
    t = m // batch
    tb = _pick_tile(t, (256, 128, 64))
    nt = t // tb
    conv_w, al, dtb, gn = params
    const = lambda b, i: (0, 0)
    nb = DN_CONV - 1
    return pl.pallas_call(
        _dn_prompt_kernel,
        grid=(batch, nt),
        in_specs=[pl.BlockSpec((tb, width), lambda b, i: (b * nt + i, 0)),
                  pl.BlockSpec((DN_CONV, DN_CONV_CH), const),
                  pl.BlockSpec((1, LANES), const),
                  pl.BlockSpec((1, LANES), const),
                  pl.BlockSpec((1, DN_HEAD), const)],
        out_specs=[pl.BlockSpec((tb, DN_MIX), lambda b, i: (b * nt + i, 0)),
                   pl.BlockSpec((1, DN_HEADS, DN_HEAD, DN_HEAD), lambda b, i: (b, 0, 0, 0)),
                   pl.BlockSpec((1, nb, DN_CONV_CH), lambda b, i: (b, 0, 0))],
        out_shape=[jax.ShapeDtypeStruct((m, DN_MIX), BF16),
                   jax.ShapeDtypeStruct((batch, DN_HEADS, DN_HEAD, DN_HEAD), F32),
                   jax.ShapeDtypeStruct((batch, nb, DN_CONV_CH), F32)],
        scratch_shapes=[pltpu.VMEM((DN_HEADS, DN_HEAD, DN_HEAD), F32),
                        pltpu.VMEM((nb, DN_CONV_CH), F32),
                        pltpu.VMEM((tb, DN_CONV_CH), F32),
                        pltpu.VMEM((2, tb, LANES), F32),
                        pltpu.VMEM((4, tb // DN_CHUNK, DN_HEADS, DN_CHUNK, DN_HEAD), F32),
                        pltpu.VMEM((tb // DN_CHUNK, DN_HEADS, DN_CHUNK, DN_CHUNK), F32),
                        pltpu.VMEM((tb // DN_CHUNK, DN_HEADS, 1, DN_HEAD), F32)],
        compiler_params=_cparams("parallel", "arbitrary"),
        name="dn_prompt",
    )(d_cols, conv_w, al, dtb, gn)


SB_MIX = SB_HEADS * SB_HEAD
SB_PAIR = LANES // SB_HEAD


SB_SUB = LANES
SB_TQ = 256
SB_KB = 512


def _neg_abs(x):
    bits = lax.bitcast_convert_type(x, jnp.uint32) | jnp.uint32(0x80000000)
    return lax.bitcast_convert_type(bits, F32)


def _log_sigmoid_pair(z):
    lsig = jnp.minimum(z, 0.0) - jnp.log(1.0 + jnp.exp(_neg_abs(z)))
    return lsig, lsig - z


def _sb_weights(zs, causals, runs):
    heads = range(len(zs))
    subs = range(len(zs[0]))
    tk = zs[0][0].shape[1]
    upper = (_iota((tk, tk), 0) > _iota((tk, tk), 1)).astype(BF16)
    upper2 = jnp.concatenate([upper, upper], axis=0)
    pairs = [[_log_sigmoid_pair(z) for z in zs[r]] for r in heads]
    lsig = [[p[0] for p in pairs[r]] for r in heads]
    stay = [[x[1] if causals[s] is None else jnp.where(causals[s], x[1], 0.0) for s, x in enumerate(pairs[r])]
            for r in heads]
    later = [[jnp.dot(jnp.concatenate(_split2(x), axis=1), upper2, preferred_element_type=F32)
              for x in stay[r]] for r in heads]
    sums = [[jnp.sum(x, axis=-1, keepdims=True) for x in stay[r]] for r in heads]
    carries = [[None] * len(subs) for _ in heads]
    new_runs = []
    for r in heads:
        run = runs[r]
        for s in reversed(subs):
            carries[r][s] = run
            run = run + sums[r][s]
        new_runs.append(run)
    ws = [[jnp.exp(lsig[r][s] + (later[r][s] + carries[r][s])) for s in subs] for r in heads]
    ws = [[w if causals[s] is None else jnp.where(causals[s], w, 0.0) for s, w in enumerate(ws[r])]
          for r in heads]
    return [[w.astype(BF16) for w in ws[r]] for r in heads], new_runs


def _sb_prompt_kernel(q_ref, k_ref, v_ref, b_ref, o_ref):
    tq = q_ref.shape[0]
    kb = SB_KB
    n_sub = kb // SB_SUB
    i = pl.program_id(2)
    lane = _iota((1, LANES), 1)
    masks = [(lane // SB_HEAD == r).astype(F32) for r in range(SB_PAIR)]
    q = q_ref[...] * SB_HEAD ** -0.5
    one2 = (_iota((tq, LANES), 1) < 2).astype(BF16)
    qa = [jnp.concatenate([(q * masks[r]).astype(BF16), one2], axis=1) for r in range(SB_PAIR)]
    ktail = []
    for r in range(SB_PAIR):
        b_hi, b_lo = _split2(b_ref[0, r:r + 1, :])
        sel = _iota((kb, LANES), 1)
        ktail.append(jnp.where(sel == 0, b_hi, jnp.where(sel == 1, b_lo, jnp.zeros((), BF16))))
    q_pos = i * tq + _iota((tq, SB_SUB), 0)

    def block(j, diagonal, state):
        acc, runs = state
        rows = pl.ds(pl.multiple_of(j * kb, kb), kb)
        k = k_ref[rows, :].astype(BF16)
        v = v_ref[rows, :]
        zs = []
        for r in range(SB_PAIR):
            z = _dot_nt(qa[r], jnp.concatenate([k, ktail[r]], axis=1))
            zs.append([z[:, s * SB_SUB:(s + 1) * SB_SUB] for s in range(n_sub)])
        causals = [(j * kb + s * SB_SUB + _iota((tq, SB_SUB), 1) < q_pos) if diagonal else None
                   for s in range(n_sub)]
        ws, runs = _sb_weights(zs, causals, list(runs))
        for s in range(n_sub):
            vs = v[s * SB_SUB:(s + 1) * SB_SUB, :]
            v2 = jnp.concatenate([(vs * masks[r]).astype(BF16) for r in range(SB_PAIR)], axis=0)
            acc = acc + jnp.dot(jnp.concatenate([ws[r][s] for r in range(SB_PAIR)], axis=1), v2,
                                preferred_element_type=F32)
        return acc, tuple(runs)

    init = (jnp.zeros((tq, LANES), F32), tuple(jnp.zeros((tq, 1), F32) for _ in range(SB_PAIR)))
    jd = (i * tq) // kb
    state = block(jd, True, init)
    state = lax.fori_loop(0, jd, lambda step, st: block(jd - 1 - step, False, st), state)
    o_ref[...] = state[0].astype(o_ref.dtype)


def sb_prompt(q, k, v, bias, *, batch):
    m, width = q.shape
    t = m // batch
    tq = SB_TQ
    assert t % SB_KB == 0 and SB_KB % tq == 0
    nq = t // tq
    npair = SB_HEADS // SB_PAIR
    bias_rows = jnp.broadcast_to(bias.reshape(npair, SB_PAIR, 1), (npair, SB_PAIR, LANES))
    return pl.pallas_call(
        _sb_prompt_kernel,
        grid=(batch, npair, nq),
        in_specs=[pl.BlockSpec((tq, LANES), lambda b, p, i: (b * nq + i, p)),
                  pl.BlockSpec((t, LANES), lambda b, p, i: (b, p)),
                  pl.BlockSpec((t, LANES), lambda b, p, i: (b, p)),
                  pl.BlockSpec((1, SB_PAIR, LANES), lambda b, p, i: (p, 0, 0))],
        out_specs=pl.BlockSpec((tq, LANES), lambda b, p, i: (b * nq + i, p)),
        out_shape=jax.ShapeDtypeStruct((m, width), BF16),
        compiler_params=_cparams("parallel", "parallel", "parallel"),
        name="sb_prompt",
    )(q, k, v, bias_rows)


STEP_ROWS = 8


def _gla_step_kernel(g_ref, wa_ref, ba_ref, gn_ref, s_ref, o_ref, sout_ref):
    nk = GLA_HEADS * GLA_DK
    half = GLA_HEADS * GLA_DV
    lr_off = 2 * nk + 2 * half
    lf = _log_sigmoid(_dot(g_ref[:, lr_off:lr_off + LANES], wa_ref[...]) + ba_ref[...]) * (1.0 / GLA_GATE_NORM)
    decay = jnp.exp(lf)
    for b in range(g_ref.shape[0]):
        for h in range(GLA_HEADS):
            q_col = _row_to_col(g_ref[b:b + 1, h * GLA_DK:(h + 1) * GLA_DK] * GLA_DK ** -0.5)
            k_col = _row_to_col(g_ref[b:b + 1, nk + h * GLA_DK:nk + (h + 1) * GLA_DK])
            f_col = _row_to_col(decay[b:b + 1, h * GLA_DK:(h + 1) * GLA_DK])
            v = g_ref[b:b + 1, 2 * nk + h * GLA_DV:2 * nk + (h + 1) * GLA_DV]
            gate = g_ref[b:b + 1, 2 * nk + half + h * GLA_DV:2 * nk + half + (h + 1) * GLA_DV]
            s_new = f_col * s_ref[0, b, h] + k_col * v
            sout_ref[b, h] = s_new
            o = jnp.sum(q_col * s_new, axis=0, keepdims=True)
            o_ref[b:b + 1, h * GLA_DV:(h + 1) * GLA_DV] = _rms(o, gn_ref[...]) * _silu(gate)


def gla_step(g_cols, w_a2, b_a, gnorm, state, *, layer):
    db, width = g_cols.shape
    nk = GLA_HEADS * GLA_DK
    half = GLA_HEADS * GLA_DV
    wa = jnp.zeros((LANES, nk), F32).at[:GLA_LR].set(w_a2)
    nb = STEP_ROWS
    return pl.pallas_call(
        _gla_step_kernel,
        grid=(db // nb,),
        in_specs=[pl.BlockSpec((nb, width), lambda i: (i, 0)),
                  pl.BlockSpec((LANES, nk), lambda i: (0, 0)),
                  pl.BlockSpec((1, nk), lambda i: (0, 0)),
                  pl.BlockSpec((1, GLA_DV), lambda i: (0, 0)),
                  pl.BlockSpec((1, nb, GLA_HEADS, GLA_DK, GLA_DV), lambda i: (layer, i, 0, 0, 0))],
        out_specs=[pl.BlockSpec((nb, half), lambda i: (i, 0)),
                   pl.BlockSpec((nb, GLA_HEADS, GLA_DK, GLA_DV), lambda i: (i, 0, 0, 0))],
        out_shape=[jax.ShapeDtypeStruct((db, half), F32),
                   jax.ShapeDtypeStruct((db, GLA_HEADS, GLA_DK, GLA_DV), F32)],
        compiler_params=_cparams("parallel"),
        name="gla_step",
    )(g_cols, wa, b_a.reshape(1, nk), gnorm.reshape(1, GLA_DV), state)


def _rwkv_step_kernel(r_ref, prev_ref, mu_ref, lw_ref, la_ref, g2_ref, tab_ref, s_ref, o_ref, sout_ref,
                      f_ref, ft_ref, yt_ref):
    h = pl.program_id(0)
    hd = RWKV_HEAD
    tab = tab_ref[...]

    @pl.when(h == 0)
    def _():
        rr, k2, rv, kk, a, ld, g = _rwkv_prep(r_ref[...], prev_ref[0], mu_ref[...], lw_ref[...], la_ref[...],
                                              g2_ref[...], tab)
        for n, x in enumerate((rr, k2, rv, g)):
            f_ref[n] = x
        for n, x in enumerate((-kk, jnp.exp(ld), kk * a, k2, rr, rv)):
            ft_ref[n] = x.T

    base = pl.multiple_of(h * hd, hd)
    nkk, w, kka, k2t, rt = (ft_ref[n, pl.ds(base, hd), :] for n in range(5))

    def row(i, carry):
        s = s_ref[0, 0, i]
        sa = jnp.sum(s * nkk, axis=0, keepdims=True)
        s_new = s * w + sa * kka + ft_ref[5, pl.ds(base + i, 1), :] * k2t
        sout_ref[0, i] = s_new
        yt_ref[pl.ds(base + i, 1), :] = jnp.sum(s_new * rt, axis=0, keepdims=True)
        return carry

    lax.fori_loop(0, hd, row, 0)

    @pl.when(h == pl.num_programs(0) - 1)
    def _():
        o_ref[...] = _rwkv_finish(yt_ref[...].T, f_ref[0], f_ref[1], f_ref[2], f_ref[3], tab)


def rwkv_step(r_cols, params, state, shift, *, layer):
    db, width = r_cols.shape
    mix = RWKV_MIX
    mu, lora_w, lora_a, g2, tab = params
    hd = RWKV_HEAD
    const = lambda h: (0, 0)
    y, s_t = pl.pallas_call(
        _rwkv_step_kernel,
        grid=(RWKV_HEADS,),
        in_specs=[pl.BlockSpec((db, width), const),
                  pl.BlockSpec((1, db, width), lambda h: (layer, 0, 0)),
                  pl.BlockSpec((1, width), const),
                  pl.BlockSpec((LANES, mix), const),
                  pl.BlockSpec((LANES, mix), const),
                  pl.BlockSpec((RWKV_G_LR, mix), const),
                  pl.BlockSpec((RWKV_PARAM_ROWS, mix), const),
                  pl.BlockSpec((1, 1, hd, hd, db), lambda h: (layer, h, 0, 0, 0))],
        out_specs=[pl.BlockSpec((db, mix), const),
                   pl.BlockSpec((1, hd, hd, db), lambda h: (h, 0, 0, 0))],
        out_shape=[jax.ShapeDtypeStruct((db, mix), F32),
                   jax.ShapeDtypeStruct((RWKV_HEADS, hd, hd, db), F32)],
        scratch_shapes=[pltpu.VMEM((4, db, mix), F32),
                        pltpu.VMEM((6, mix, db), F32),
                        pltpu.VMEM((mix, db), F32)],
        compiler_params=_cparams("arbitrary"),
        name="rwkv_step",
    )(r_cols, shift, mu, lora_w, lora_a, g2, tab, jnp.transpose(state, (0, 2, 3, 4, 1)))
    return y, jnp.transpose(s_t, (3, 0, 1, 2))


def _dn_step_kernel(c_ref, cw_ref, al_ref, dtb_ref, gn_ref, s_ref, buf_ref, o_ref, sout_ref, cout_ref):
    hd = DN_HEAD
    nb = DN_CONV - 1
    cw = cw_ref[...]
    gdec, beta = _dn_gates(c_ref[:, DN_GATE_OFF:DN_GATE_OFF + LANES], al_ref[...], dtb_ref[...])
    decay = jnp.exp(gdec)
    for b in range(c_ref.shape[0]):
        x = c_ref[b:b + 1, :DN_CONV_CH]
        before = buf_ref[0, b]
        y = x * cw[nb:nb + 1, :]
        for k in range(nb):
            y = y + before[k:k + 1, :] * cw[k:k + 1, :]
        cout_ref[b] = jnp.concatenate([before[1:nb, :], x], axis=0)
        xc = _silu(y)
        for h in range(DN_HEADS):
            q = _l2n(xc[:, h * hd:(h + 1) * hd]) * hd ** -0.5
            k = _l2n(xc[:, DN_MIX + h * hd:DN_MIX + (h + 1) * hd])
            v = xc[:, 2 * DN_MIX + h * hd:2 * DN_MIX + (h + 1) * hd]
            dec = decay[b:b + 1, h:h + 1]
            bet = beta[b:b + 1, DN_HEADS + h:DN_HEADS + h + 1]
            s = s_ref[0, b, h]
            k_col = _row_to_col(k)
            u = bet * (v - dec * jnp.sum(k_col * s, axis=0, keepdims=True))
            s_new = dec * s + k_col * u
            sout_ref[b, h] = s_new
            o = jnp.sum(_row_to_col(q) * s_new, axis=0, keepdims=True)
            z = c_ref[b:b + 1, DN_CONV_CH + h * hd:DN_CONV_CH + (h + 1) * hd]
            o_ref[b:b + 1, h * hd:(h + 1) * hd] = _rms(o, gn_ref[...]) * _silu(z)


def dn_step(d_cols, params, state, conv_buf, *, layer):
    db, width = d_cols.shape
    conv_w, al, dtb, gn = params
    nb = STEP_ROWS
    nc = DN_CONV - 1
    const = lambda i: (0, 0)
    return pl.pallas_call(
        _dn_step_kernel,
        grid=(db // nb,),
        in_specs=[pl.BlockSpec((nb, width), lambda i: (i, 0)),
                  pl.BlockSpec((DN_CONV, DN_CONV_CH), const),
                  pl.BlockSpec((1, LANES), const),
                  pl.BlockSpec((1, LANES), const),
                  pl.BlockSpec((1, DN_HEAD), const),
                  pl.BlockSpec((1, nb, DN_HEADS, DN_HEAD, DN_HEAD), lambda i: (layer, i, 0, 0, 0)),
                  pl.BlockSpec((1, nb, nc, DN_CONV_CH), lambda i: (layer, i, 0, 0))],
        out_specs=[pl.BlockSpec((nb, DN_MIX), lambda i: (i, 0)),
                   pl.BlockSpec((nb, DN_HEADS, DN_HEAD, DN_HEAD), lambda i: (i, 0, 0, 0)),
                   pl.BlockSpec((nb, nc, DN_CONV_CH), lambda i: (i, 0, 0))],
        out_shape=[jax.ShapeDtypeStruct((db, DN_MIX), F32),
                   jax.ShapeDtypeStruct((db, DN_HEADS, DN_HEAD, DN_HEAD), F32),
                   jax.ShapeDtypeStruct((db, nc, DN_CONV_CH), F32)],
        compiler_params=_cparams("parallel"),
        name="dn_step",
    )(d_cols, conv_w, al, dtb, gn, state, conv_buf)


def _sb_step_kernel(pt_ref, q_ref, b_ref, *refs, n_pages):
    del pt_ref
    k_refs = refs[:n_pages]
    v_refs = refs[n_pages:2 * n_pages]
    o_ref = refs[2 * n_pages]
    width = q_ref.shape[2]
    head_mask = (_iota((SB_HEADS, width), 0) == _iota((SB_HEADS, width), 1) // SB_HEAD).astype(F32)
    qh = (q_ref[0] * SB_HEAD ** -0.5 * head_mask).astype(BF16)
    bias = b_ref[:, 0:1]
    pages = range(n_pages)
    page = k_refs[0].shape[3]
    upper = (_iota((page, page), 0) > _iota((page, page), 1)).astype(BF16)
    zs = [_dot(qh, k_refs[p][0, 0]) + bias for p in pages]
    stay = [jnp.minimum(-z, 0.0) - jnp.log(1.0 + jnp.exp(-jnp.abs(z))) for z in zs]
    parts = [_split2(x) for x in stay]
    later = [jnp.dot(hi, upper, preferred_element_type=F32) + jnp.dot(lo, upper, preferred_element_type=F32)
             for hi, lo in parts]
    sums = [jnp.sum(x, axis=-1, keepdims=True) for x in stay]
    runs = [None] * n_pages
    run = jnp.zeros((SB_HEADS, 1), F32)
    for p in reversed(pages):
        runs[p] = run
        run = run + sums[p]
    ws = [jnp.exp(zs[p] + stay[p] + (later[p] + runs[p])) for p in pages]
    acc = sum(_dot_nt(ws[p], v_refs[p][0, 0]) for p in pages)
    o_ref[0] = jnp.sum(acc * head_mask, axis=0, keepdims=True)


def sb_step(q, bias, cache_k, cache_v, page_table, *, layer):
    db, width = q.shape
    n_layers, n_phys, page = cache_k.shape[:3]
    n_pages = page_table.shape[1]
    bias_rows = jnp.broadcast_to(bias.reshape(SB_HEADS, 1), (SB_HEADS, LANES))

    def view(c):
        return jnp.transpose(c, (0, 1, 3, 4, 2)).reshape(n_layers, n_phys, width, page)

    def page_spec(p):
        return pl.BlockSpec((1, 1, width, page), lambda b, pt: (layer, pt[b, p], 0, 0))

    grid_spec = pltpu.PrefetchScalarGridSpec(
        num_scalar_prefetch=1,
        grid=(db,),
        in_specs=[pl.BlockSpec((1, 1, width), lambda b, pt: (b, 0, 0)),
                  pl.BlockSpec((SB_HEADS, LANES), lambda b, pt: (0, 0))]
                 + [page_spec(p) for p in range(n_pages)] * 2,
        out_specs=pl.BlockSpec((1, 1, width), lambda b, pt: (b, 0, 0)))
    out = pl.pallas_call(
        functools.partial(_sb_step_kernel, n_pages=n_pages),
        grid_spec=grid_spec,
        out_shape=jax.ShapeDtypeStruct((db, 1, width), F32),
        compiler_params=_cparams("parallel"),
        name="sb_step",
    )(page_table, q.reshape(db, 1, width), bias_rows,
      *([view(cache_k)] * n_pages), *([view(cache_v)] * n_pages))
    return out.reshape(db, width)


PROJ_PAD = 256


def _pad_cols(w, mult):
    n = w.shape[1]
    return jnp.pad(w, ((0, 0), (0, _round_up(n, mult) - n)))


def kernel(x_prompt, x_sample, mem_prompt, state_gla, state_rwkv, state_rwkv_shift, cache_sb_k, cache_sb_v, page_table, state_delta, state_delta_conv, cache_mem_k, cache_mem_v, norm_mix, norm_cross, norm_ffn, w_in_even, w_out_even, gla_w_a2, gla_b_a, gla_norm, rwkv_mu, rwkv_w0, rwkv_w2, rwkv_a0, rwkv_a2, rwkv_g2, rwkv_k_k, rwkv_k_a, rwkv_r_k, rwkv_ln_w, rwkv_ln_b, w_in_odd, w_out_odd, sb_q_norm, sb_k_norm, sb_bias, dn_conv, dn_a_log, dn_dt_bias, dn_norm, mem_norm, w_xq, w_xkv, w_xo, xq_norm, xk_norm, w_gu, w_down):
    b, t, d = x_prompt.shape
    db = x_sample.shape[0]
    depth = norm_mix.shape[0]
    mem_len = mem_prompt.shape[1]
    xh = d // X_HEADS
    gla_cols = 2 * GLA_HEADS * GLA_DK + 2 * GLA_HEADS * GLA_DV + GLA_LR

    xp = x_prompt.reshape(b * t, d)
    xs = x_sample.reshape(db, d)
    mem = mem_prompt.reshape(b * mem_len, d)

    p_gla, p_rwkv, p_shift, p_sbk, p_sbv, p_dn, p_conv, p_mk, p_mv = ([] for _ in range(9))
    s_gla, s_rwkv, s_shift, s_sbk, s_sbv, s_dn, s_conv = ([] for _ in range(7))

    for i in range(depth):
        j = i // 2
        if i % 2 == 0:
            w_in = w_in_even[j].astype(BF16)
            w_g = _pad_cols(w_in[:, :gla_cols], PROJ_PAD)
            w_r = w_in[:, gla_cols:]
            w_out = w_out_even[j].astype(BF16)
            wo_a, wo_b = w_out[:GLA_HEADS * GLA_DV], w_out[GLA_HEADS * GLA_DV:]
            rp = _rwkv_params(rwkv_mu[j], rwkv_w0[j], rwkv_w2[j], rwkv_a0[j], rwkv_a2[j], rwkv_g2[j],
                              rwkv_k_k[j], rwkv_k_a[j], rwkv_r_k[j], rwkv_ln_w[j], rwkv_ln_b[j])
            g_cols = norm_matmul(xp, norm_mix[i], w_g)
            r_cols = norm_matmul(xp, norm_mix[i], w_r)
            o_a, sg = gla_prompt(g_cols, gla_w_a2[j], gla_b_a[j], gla_norm[j], batch=b)
            o_b, sr, sh = rwkv_prompt(r_cols, rp, batch=b)
            xp = matmul_residual([(o_a, wo_a), (o_b, wo_b)], xp)
            p_gla.append(sg)
            p_rwkv.append(sr)
            p_shift.append(sh.reshape(b, -1))
            g_cols = norm_matmul(xs, norm_mix[i], w_g)
            r_cols = norm_matmul(xs, norm_mix[i], w_r)
            o_a, sg = gla_step(g_cols, gla_w_a2[j], gla_b_a[j], gla_norm[j], state_gla, layer=j)
            o_b, sr = rwkv_step(r_cols, rp, state_rwkv, state_rwkv_shift, layer=j)
            xs = matmul_residual([(o_a, wo_a), (o_b, wo_b)], xs)
            s_gla.append(sg)
            s_rwkv.append(sr)
            s_shift.append(r_cols)
        else:
            w_in = w_in_odd[j].astype(BF16)
            w_q, w_k, w_v = (w_in[:, n * SB_MIX:(n + 1) * SB_MIX] for n in range(3))
            w_d = _pad_cols(w_in[:, 3 * SB_MIX:], PROJ_PAD)
            w_out = w_out_odd[j].astype(BF16)
            wo_a, wo_b = w_out[:SB_MIX], w_out[SB_MIX:]
            qn = jnp.tile(sb_q_norm[j], SB_HEADS)
            kn = jnp.tile(sb_k_norm[j], SB_HEADS)
            dp = _dn_params(dn_conv[j], dn_a_log[j], dn_dt_bias[j], dn_norm[j])
            q = norm_matmul(xp, norm_mix[i], w_q, head_gain=qn, head_group=SB_HEAD)
            k = norm_matmul(xp, norm_mix[i], w_k, head_gain=kn, head_group=SB_HEAD)
            v = norm_matmul(xp, norm_mix[i], w_v)
            d_cols = norm_matmul(xp, norm_mix[i], w_d)
            o_a = sb_prompt(q, k, v, sb_bias[j], batch=b)
            o_b, sd, sc = dn_prompt(d_cols, dp, batch=b)
            xp = matmul_residual([(o_a, wo_a), (o_b, wo_b)], xp)
            p_sbk.append(k.reshape(b, t, SB_HEADS, SB_HEAD))
            p_sbv.append(v.reshape(b, t, SB_HEADS, SB_HEAD))
            p_dn.append(sd)
            p_conv.append(sc)
            q = norm_matmul(xs, norm_mix[i], w_q, head_gain=qn, head_group=SB_HEAD)
            k = norm_matmul(xs, norm_mix[i], w_k, head_gain=kn, head_group=SB_HEAD)
            v = norm_matmul(xs, norm_mix[i], w_v)
            d_cols = norm_matmul(xs, norm_mix[i], w_d)
            o_a = sb_step(q, sb_bias[j], cache_sb_k, cache_sb_v, page_table, layer=j)
            o_b, sd, sc = dn_step(d_cols, dp, state_delta, state_delta_conv, layer=j)
            xs = matmul_residual([(o_a, wo_a), (o_b, wo_b)], xs)
            s_sbk.append(k.reshape(db, 1, SB_HEADS, SB_HEAD))
            s_sbv.append(v.reshape(db, 1, SB_HEADS, SB_HEAD))
            s_dn.append(sd)
            s_conv.append(sc)

        w_kv = w_xkv[i].astype(BF16)
        wq = w_xq[i].astype(BF16)
        wo = w_xo[i].astype(BF16)
        mk = norm_matmul(mem, mem_norm[i], w_kv[:, :d], head_gain=jnp.tile(xk_norm[i], X_HEADS), head_group=xh)
        mv = norm_matmul(mem, mem_norm[i], w_kv[:, d:])
        p_mk.append(mk.reshape(b, mem_len, X_HEADS, xh))
        p_mv.append(mv.reshape(b, mem_len, X_HEADS, xh))
        xp = cross_attn_prompt(xp, norm_cross[i], wq, xq_norm[i], mk, mv, wo, batch=b)
        q = norm_matmul(xs, norm_cross[i], wq, head_gain=jnp.tile(xq_norm[i], X_HEADS), head_group=xh)
        xs = matmul_residual([(cross_attn_sample(q, cache_mem_k, cache_mem_v, layer=i), wo)], xs)

        wgu = w_gu[i].astype(BF16)
        wdn = w_down[i].astype(BF16)
        xp = ffn(xp, norm_ffn[i], wgu, wdn)
        xs = ffn(xs, norm_ffn[i], wgu, wdn)

    return (xp.reshape(b, t, d), xs.reshape(db, 1, d),
            jnp.stack(p_gla), jnp.stack(p_rwkv), jnp.stack(p_shift),
            jnp.stack(p_sbk), jnp.stack(p_sbv), jnp.stack(p_dn), jnp.stack(p_conv),
            jnp.stack(p_mk), jnp.stack(p_mv),
            jnp.stack(s_gla), jnp.stack(s_rwkv), jnp.stack(s_shift),
            jnp.stack(s_sbk), jnp.stack(s_sbv), jnp.stack(s_dn), jnp.stack(s_conv))
```

```python
import functools
import math

import jax
import jax.numpy as jnp
from jax import lax
from jax.experimental import pallas as pl
from jax.experimental.pallas import tpu as pltpu

F32 = jnp.float32
BF16 = jnp.bfloat16

LANES = 128
SUBLANES = 8
VMEM_LIMIT_BYTES = 56 * 1024 * 1024
MAX_TN = 2304

EPS = 1e-6
PAGE_SIZE = 128

GLA_HEADS, GLA_DK, GLA_DV, GLA_LR = 4, 64, 128, 16
GLA_GATE_NORM = 16.0
GLA_CHUNK = 64
GLA_SUB = 8
RWKV_HEADS, RWKV_HEAD = 8, 64
RWKV_W_LR, RWKV_A_LR, RWKV_G_LR = 64, 64, 128
RWKV_LN_EPS = 64e-5
RWKV_CHUNK = 64
SB_HEADS, SB_HEAD = 8, 64
DN_HEADS, DN_HEAD, DN_CONV = 4, 128, 4
DN_CHUNK = 64
X_HEADS = 4


def _cparams(*sem):
    return pltpu.CompilerParams(dimension_semantics=sem, vmem_limit_bytes=VMEM_LIMIT_BYTES)


def _round_up(n, m):
    return -(-n // m) * m


def _pick_tile(n, candidates):
    for c in candidates:
        if n % c == 0:
            return c
    return n


def _dot(a, b):
    return jnp.dot(a.astype(BF16), b.astype(BF16), preferred_element_type=F32)


def _dot_nt(a, b):
    return lax.dot_general(a.astype(BF16), b.astype(BF16), (((1,), (1,)), ((), ())),
                           preferred_element_type=F32)


def _dot_tn(a, b):
    return lax.dot_general(a.astype(BF16), b.astype(BF16), (((0,), (0,)), ((), ())),
                           preferred_element_type=F32)


def _split3(a):
    hi = a.astype(BF16)
    r1 = a - hi.astype(F32)
    mid = r1.astype(BF16)
    lo = (r1 - mid.astype(F32)).astype(BF16)
    return hi, mid, lo


def _dot_x3(a, b):
    b = b.astype(BF16)
    hi, mid, lo = _split3(a)
    return (jnp.dot(hi, b, preferred_element_type=F32) + jnp.dot(mid, b, preferred_element_type=F32)
            + jnp.dot(lo, b, preferred_element_type=F32))


def _dot_3x(a, b):
    a = a.astype(BF16)
    hi, mid, lo = _split3(b)
    return (jnp.dot(a, hi, preferred_element_type=F32) + jnp.dot(a, mid, preferred_element_type=F32)
            + jnp.dot(a, lo, preferred_element_type=F32))


def _split2(a):
    hi = a.astype(BF16)
    return hi, (a - hi.astype(F32)).astype(BF16)


def _dot_hp(a, b):
    a_hi, a_lo = _split2(a)
    b_hi, b_lo = _split2(b)
    return (jnp.dot(a_hi, b_hi, preferred_element_type=F32)
            + (jnp.dot(a_hi, b_lo, preferred_element_type=F32) + jnp.dot(a_lo, b_hi, preferred_element_type=F32)))


def _iota(shape, dim):
    return lax.broadcasted_iota(jnp.int32, shape, dim)


def _group_ones(n, group):
    return (_iota((n, n), 0) // group == _iota((n, n), 1) // group).astype(F32)


def _group_sum(x, group):
    width = x.shape[-1]
    slab = min(width, max(group, LANES))
    ones = _group_ones(slab, group)
    parts = [_dot_x3(x[:, s:s + slab], ones) for s in range(0, width, slab)]
    return parts[0] if len(parts) == 1 else jnp.concatenate(parts, axis=-1)


def _rms(x, gain):
    return x * lax.rsqrt(jnp.mean(x * x, axis=-1, keepdims=True) + EPS) * gain


def _group_rms(x, gain_row, group):
    ms = _group_sum(x * x, group) * (1.0 / group)
    return x * lax.rsqrt(ms + EPS) * gain_row


def _silu(x):
    return x * jax.nn.sigmoid(x)


def _softplus(x):
    return jnp.maximum(x, 0.0) + jnp.log1p(jnp.exp(-jnp.abs(x)))


def _log_sigmoid(x):
    return -_softplus(-x)


def _norm_matmul_kernel(x_ref, g_ref, w_ref, *rest, head_group):
    if head_group:
        hg_ref, o_ref, h_ref = rest
    else:
        o_ref, h_ref = rest

    @pl.when(pl.program_id(1) == 0)
    def _():
        h_ref[...] = _rms(x_ref[...], g_ref[...]).astype(BF16)

    y = jnp.dot(h_ref[...], w_ref[...], preferred_element_type=F32)
    if head_group:
        y = _group_rms(y, hg_ref[...], head_group)
    o_ref[...] = y.astype(o_ref.dtype)


def norm_matmul(x, gain, w, *, head_gain=None, head_group=0, out_dtype=F32):
    m, k = x.shape
    n = w.shape[1]
    tm = _pick_tile(m, (1024, 512, 256, 128))
    tn = n if n <= MAX_TN else _pick_tile(n, (1024, 768, 512, 384, 256, 128))
    in_specs = [pl.BlockSpec((tm, k), lambda i, j: (i, 0)),
                pl.BlockSpec((1, k), lambda i, j: (0, 0)),
                pl.BlockSpec((k, tn), lambda i, j: (0, j))]
    args = [x, gain.reshape(1, k), w]
    if head_group:
        in_specs.append(pl.BlockSpec((1, tn), lambda i, j: (0, j)))
        args.append(head_gain.reshape(1, n))
    return pl.pallas_call(
        functools.partial(_norm_matmul_kernel, head_group=head_group),
        grid=(m // tm, n // tn),
        in_specs=in_specs,
        out_specs=pl.BlockSpec((tm, tn), lambda i, j: (i, j)),
        out_shape=jax.ShapeDtypeStruct((m, n), out_dtype),
        scratch_shapes=[pltpu.VMEM((tm, k), BF16)],
        compiler_params=_cparams("parallel", "arbitrary"),
        name="norm_matmul",
    )(*args)


def _matmul_res_kernel(*refs, n_pairs):
    res_ref = refs[2 * n_pairs]
    o_ref = refs[2 * n_pairs + 1]
    acc = res_ref[...]
    for p in range(n_pairs):
        acc = acc + jnp.dot(refs[2 * p][...].astype(BF16), refs[2 * p + 1][...],
                            preferred_element_type=F32)
    o_ref[...] = acc


def matmul_residual(pairs, res):
    m, n = res.shape
    tm = _pick_tile(m, (1024, 512, 256, 128))
    tn = n if n <= MAX_TN else _pick_tile(n, (1024, 512, 256, 128))
    in_specs, args = [], []
    for a, w in pairs:
        kk = a.shape[1]
        in_specs.append(pl.BlockSpec((tm, kk), lambda i, j: (i, 0)))
        in_specs.append(pl.BlockSpec((kk, tn), lambda i, j: (0, j)))
        args += [a, w]
    in_specs.append(pl.BlockSpec((tm, tn), lambda i, j: (i, j)))
    args.append(res)
    return pl.pallas_call(
        functools.partial(_matmul_res_kernel, n_pairs=len(pairs)),
        grid=(m // tm, n // tn),
        in_specs=in_specs,
        out_specs=pl.BlockSpec((tm, tn), lambda i, j: (i, j)),
        out_shape=jax.ShapeDtypeStruct((m, n), F32),
        compiler_params=_cparams("parallel", "parallel"),
        name="matmul_residual",
    )(*args)


FFN_CHUNK = 256


def _ffn_kernel(x_ref, g_ref, wgu_ref, wd_ref, o_ref):
    x = x_ref[...]
    ff = wd_ref.shape[0]
    h = _rms(x, g_ref[...]).astype(BF16)
    acc = x
    for f in range(0, ff, FFN_CHUNK):
        gate = jnp.dot(h, wgu_ref[:, f:f + FFN_CHUNK], preferred_element_type=F32)
        up = jnp.dot(h, wgu_ref[:, ff + f:ff + f + FFN_CHUNK], preferred_element_type=F32)
        act = (_silu(gate) * up).astype(BF16)
        acc = acc + jnp.dot(act, wd_ref[f:f + FFN_CHUNK, :], preferred_element_type=F32)
    o_ref[...] = acc


def ffn(x, gain, w_gu, w_down):
    m, d = x.shape
    ff = w_down.shape[0]
    assert ff % FFN_CHUNK == 0
    tm = _pick_tile(m, (512, 256, 128))
    resident = pl.Buffered(1)
    return pl.pallas_call(
        _ffn_kernel,
        grid=(m // tm,),
        in_specs=[pl.BlockSpec((tm, d), lambda i: (i, 0)),
                  pl.BlockSpec((1, d), lambda i: (0, 0)),
                  pl.BlockSpec((d, 2 * ff), lambda i: (0, 0), pipeline_mode=resident),
                  pl.BlockSpec((ff, d), lambda i: (0, 0), pipeline_mode=resident)],
        out_specs=pl.BlockSpec((tm, d), lambda i: (i, 0)),
        out_shape=jax.ShapeDtypeStruct((m, d), F32),
        compiler_params=_cparams("parallel"),
        name="ffn",
    )(x, gain.reshape(1, d), w_gu, w_down)


def _softmax_rows(s):
    p = jnp.exp(s - jnp.max(s, axis=-1, keepdims=True))
    return p / jnp.sum(p, axis=-1, keepdims=True)


def _cross_prompt_kernel(x_ref, g_ref, wq_ref, qg_ref, mk_ref, mv_ref, wo_ref, o_ref, *, heads):
    x = x_ref[...]
    d = x.shape[1]
    dh = d // heads
    h = _rms(x, g_ref[...]).astype(BF16)
    q = jnp.dot(h, wq_ref[...], preferred_element_type=F32)
    mk = mk_ref[...].astype(BF16)
    mv = mv_ref[...].astype(BF16)
    outs = []
    for hh in range(heads):
        sl = slice(hh * dh, (hh + 1) * dh)
        qh = _rms(q[:, sl], qg_ref[...])
        s = _dot_nt(qh, mk[:, sl]) * dh ** -0.5
        outs.append(_dot(_softmax_rows(s), mv[:, sl]))
    o = jnp.concatenate(outs, axis=-1)
    o_ref[...] = x + _dot(o, wo_ref[...])


def cross_attn_prompt(x, gain, w_xq, xq_norm, mk, mv, w_xo, *, batch):
    m, d = x.shape
    t = m // batch
    mem_len = mk.shape[0] // batch
    dh = d // X_HEADS
    tm = _pick_tile(t, (512, 256, 128))
    nt = t // tm
    return pl.pallas_call(
        functools.partial(_cross_prompt_kernel, heads=X_HEADS),
        grid=(batch, nt),
        in_specs=[pl.BlockSpec((tm, d), lambda b, i: (b * nt + i, 0)),
                  pl.BlockSpec((1, d), lambda b, i: (0, 0)),
                  pl.BlockSpec((d, d), lambda b, i: (0, 0)),
                  pl.BlockSpec((1, dh), lambda b, i: (0, 0)),
                  pl.BlockSpec((mem_len, d), lambda b, i: (b, 0)),
                  pl.BlockSpec((mem_len, d), lambda b, i: (b, 0)),
                  pl.BlockSpec((d, d), lambda b, i: (0, 0))],
        out_specs=pl.BlockSpec((tm, d), lambda b, i: (b * nt + i, 0)),
        out_shape=jax.ShapeDtypeStruct((m, d), F32),
        compiler_params=_cparams("parallel", "parallel"),
        name="cross_attn_prompt",
    )(x, gain.reshape(1, d), w_xq, xq_norm.reshape(1, dh), mk, mv, w_xo)


def _cross_sample_kernel(q_ref, mk_ref, mv_ref, o_ref, *, heads, mem_len):
    nb = q_ref.shape[0]
    d = q_ref.shape[2]
    dh = d // heads
    halves = dh // LANES
    assert heads * halves == SUBLANES
    row = _iota((SUBLANES, LANES), 0)
    for b in range(nb):
        q = q_ref[b] * dh ** -0.5
        s = jnp.zeros((SUBLANES, mem_len), F32)
        for r in range(SUBLANES):
            c, h = divmod(r, heads)
            chunk = q[:, h * dh + c * LANES:h * dh + (c + 1) * LANES]
            q_rows = jnp.where(row % heads == h, chunk, 0.0)
            s = s + _dot_nt(q_rows, mk_ref[0, b, pl.ds(r, mem_len, stride=SUBLANES), :])
        a = _softmax_rows(s)
        out = jnp.zeros((SUBLANES, LANES), F32)
        for r in range(SUBLANES):
            o_r = _dot(a, mv_ref[0, b, pl.ds(r, mem_len, stride=SUBLANES), :])
            out = out + jnp.where(row == r, o_r, 0.0)
        o_ref[b] = out


def cross_attn_sample(q, cache_k, cache_v, *, layer):
    db, d = q.shape
    depth, _, mem_len, heads, dh = cache_k.shape
    halves = dh // LANES

    def view(c):
        c = c.reshape(depth, db, mem_len, heads, halves, LANES)
        return c.transpose(0, 1, 2, 4, 3, 5).reshape(depth, db, mem_len * halves * heads, LANES)

    nb = _pick_tile(db, (4, 2, 1))
    rows = mem_len * halves * heads
    out = pl.pallas_call(
        functools.partial(_cross_sample_kernel, heads=heads, mem_len=mem_len),
        grid=(db // nb,),
        in_specs=[pl.BlockSpec((nb, 1, d), lambda i: (i, 0, 0)),
                  pl.BlockSpec((1, nb, rows, LANES), lambda i: (layer, i, 0, 0)),
                  pl.BlockSpec((1, nb, rows, LANES), lambda i: (layer, i, 0, 0))],
        out_specs=pl.BlockSpec((nb, halves * heads, LANES), lambda i: (i, 0, 0)),
        out_shape=jax.ShapeDtypeStruct((db, halves * heads, LANES), F32),
        compiler_params=_cparams("parallel"),
        name="cross_attn_sample",
    )(q.reshape(db, 1, d), view(cache_k), view(cache_v))
    return out.reshape(db, halves, heads, LANES).transpose(0, 2, 1, 3).reshape(db, d)


def _row_to_col(row):
    n = row.shape[1]
    eye = (_iota((n, n), 0) == _iota((n, n), 1)).astype(F32)
    return jnp.sum(eye * row, axis=-1, keepdims=True)


def _tri(n, strict=False):
    r, c = _iota((n, n), 0), _iota((n, n), 1)
    return ((r > c) if strict else (r >= c)).astype(F32)


GLA_PAIR = LANES // GLA_DK


def _gla_chunk_pairs(qs, ks, bs, vs, states):
    c = qs[0].shape[0]
    pairs = range(len(qs))
    sub = GLA_SUB
    nsub = c // sub
    lane = _iota((1, LANES), 1)
    masks = [(lane // GLA_DK == r).astype(F32) for r in range(GLA_PAIR)]
    ones = _group_ones(LANES, GLA_DK).astype(BF16)
    t_idx = _iota((c, LANES), 0)
    s_idx = _iota((c, LANES), 1) % c
    qe = [qs[p] * jnp.exp(bs[p]) for p in pairs]
    outs = [_dot(qe[p] * masks[r], states[p]) for p in pairs for r in range(GLA_PAIR)]
    rows = [[jnp.zeros((sub, LANES), F32)] for _ in pairs]
    for j in range(1, nsub):
        blk = slice(sub * j, sub * (j + 1))
        keep = _iota((sub, LANES), 1) % c < sub * j
        for p in pairs:
            ref = bs[p][sub * j - 1:sub * j, :]
            qj = qs[p][blk, :] * jnp.exp(bs[p][blk, :] - ref)
            kj = ks[p] * jnp.exp(jnp.minimum(ref - bs[p], 0.0))
            pj = jnp.concatenate([_dot_nt(qj * masks[r], kj) for r in range(GLA_PAIR)], axis=-1)
            rows[p].append(jnp.where(keep, pj, 0.0))
    pm = [jnp.concatenate(rows[p], axis=0) for p in pairs]
    for i in range(sub):
        hit = (s_idx == (t_idx // sub) * sub + i) & (t_idx % sub >= i)
        for p in pairs:
            ksel = jnp.concatenate(
                [jnp.broadcast_to(ks[p][sub * j + i:sub * j + i + 1, :], (sub, LANES)) for j in range(nsub)], axis=0)
            bsel = jnp.concatenate(
                [jnp.broadcast_to(bs[p][sub * j + i:sub * j + i + 1, :], (sub, LANES)) for j in range(nsub)], axis=0)
            hi, lo = _split2(qs[p] * ksel * jnp.exp(jnp.minimum(bs[p] - bsel, 0.0)))
            col = jnp.dot(hi, ones, preferred_element_type=F32) + jnp.dot(lo, ones, preferred_element_type=F32)
            pm[p] = jnp.where(hit, col, pm[p])
    outs = [outs[p * GLA_PAIR + r]
            + _dot(pm[p] * masks[r], jnp.concatenate([vs[p * GLA_PAIR + r]] * GLA_PAIR, axis=0))
            for p in pairs for r in range(GLA_PAIR)]
    lasts = [b[c - 1:c, :] for b in bs]
    kd = [ks[p] * jnp.exp(lasts[p] - bs[p]) for p in pairs]
    new = [_row_to_col(jnp.exp(lasts[p])) * states[p]
           + sum(_dot_tn(kd[p] * masks[r], vs[p * GLA_PAIR + r]) for r in range(GLA_PAIR)) for p in pairs]
    return outs, new


def _gla_prompt_kernel(g_ref, wa_ref, ba_ref, gn_ref, o_ref, sout_ref, s_ref):
    tb = g_ref.shape[0]
    c = GLA_CHUNK
    nk = GLA_HEADS * GLA_DK
    half = GLA_HEADS * GLA_DV
    lr_off = 2 * nk + 2 * half
    n_pairs = GLA_HEADS // GLA_PAIR
    assert GLA_PAIR * c == LANES

    @pl.when(pl.program_id(1) == 0)
    def _():
        s_ref[...] = jnp.zeros_like(s_ref)

    def chunk(ci, carry):
        rows = pl.ds(pl.multiple_of(ci * c, c), c)
        lr = g_ref[rows, lr_off:lr_off + LANES]
        lf = _log_sigmoid(_dot(lr, wa_ref[...]) + ba_ref[...]) * (1.0 / GLA_GATE_NORM)
        bcum = _dot_3x(_tri(c), lf)
        outs, new = _gla_chunk_pairs(
            [g_ref[rows, p * LANES:(p + 1) * LANES] * GLA_DK ** -0.5 for p in range(n_pairs)],
            [g_ref[rows, nk + p * LANES:nk + (p + 1) * LANES] for p in range(n_pairs)],
            [bcum[:, p * LANES:(p + 1) * LANES] for p in range(n_pairs)],
            [g_ref[rows, 2 * nk + h * GLA_DV:2 * nk + (h + 1) * GLA_DV] for h in range(GLA_HEADS)],
            [s_ref[p] for p in range(n_pairs)])
        for p in range(n_pairs):
            s_ref[p] = new[p]
        for h in range(GLA_HEADS):
            gate = g_ref[rows, 2 * nk + half + h * GLA_DV:2 * nk + half + (h + 1) * GLA_DV]
            o_ref[rows, h * GLA_DV:(h + 1) * GLA_DV] = (_rms(outs[h], gn_ref[...]) * _silu(gate)).astype(o_ref.dtype)
        return carry

    lax.fori_loop(0, tb // c, chunk, 0)
    sout_ref[0] = s_ref[...]


def gla_prompt(g_cols, w_a2, b_a, gnorm, *, batch):
    m, width = g_cols.shape
    t = m // batch
    tb = _pick_tile(t, (256, 128, 64))
    nt = t // tb
    nk = GLA_HEADS * GLA_DK
    half = GLA_HEADS * GLA_DV
    wa = jnp.zeros((LANES, nk), F32).at[:GLA_LR].set(w_a2)
    n_pairs = GLA_HEADS // GLA_PAIR
    o, s_pairs = pl.pallas_call(
        _gla_prompt_kernel,
        grid=(batch, nt),
        in_specs=[pl.BlockSpec((tb, width), lambda b, i: (b * nt + i, 0)),
                  pl.BlockSpec((LANES, nk), lambda b, i: (0, 0)),
                  pl.BlockSpec((1, nk), lambda b, i: (0, 0)),
                  pl.BlockSpec((1, GLA_DV), lambda b, i: (0, 0))],
        out_specs=[pl.BlockSpec((tb, half), lambda b, i: (b * nt + i, 0)),
                   pl.BlockSpec((1, n_pairs, LANES, GLA_DV), lambda b, i: (b, 0, 0, 0))],
        out_shape=[jax.ShapeDtypeStruct((m, half), BF16),
                   jax.ShapeDtypeStruct((batch, n_pairs, LANES, GLA_DV), F32)],
        scratch_shapes=[pltpu.VMEM((n_pairs, LANES, GLA_DV), F32)],
        compiler_params=_cparams("parallel", "arbitrary"),
        name="gla_prompt",
    )(g_cols, wa, b_a.reshape(1, nk), gnorm.reshape(1, GLA_DV))
    return o, s_pairs.reshape(batch, GLA_HEADS, GLA_DK, GLA_DV)


RWKV_MIX = RWKV_HEADS * RWKV_HEAD
RWKV_COLS = 3 * RWKV_MIX + RWKV_W_LR + RWKV_A_LR + RWKV_G_LR
RWKV_PARAM_ROWS = 8


def _unit_lower_inverses(mats):
    c = mats[0].shape[0]
    eye = (_iota((c, c), 0) == _iota((c, c), 1)).astype(F32)
    prods = [eye + m for m in mats]
    powers = list(mats)
    span = 2
    while span < c:
        powers = [_dot_hp(m, m) for m in powers]
        prods = [p + _dot_hp(p, m) for p, m in zip(prods, powers)]
        span *= 2
    return prods


def _rwkv_prep(r, prev, mu, lora_w, lora_a, g2, tab):
    m = RWKV_MIX
    xr = r + (prev - r) * mu
    rr, rk, rv = xr[:, :m], xr[:, m:2 * m], xr[:, 2 * m:3 * m]
    wa_in = xr[:, 3 * m:3 * m + LANES]
    wa_in = jnp.where(_iota(wa_in.shape, 1) < RWKV_W_LR, jnp.tanh(wa_in), wa_in)
    rg = xr[:, 3 * m + LANES:3 * m + 2 * LANES]
    w0, a0, k_k, k_a = tab[0:1], tab[1:2], tab[2:3], tab[3:4]
    w_log = -_softplus(-(w0 + _dot(wa_in, lora_w))) - 0.5
    log_decay = -jnp.exp(w_log)
    a = jax.nn.sigmoid(a0 + _dot(wa_in, lora_a))
    g = _dot(jax.nn.sigmoid(rg), g2)
    kx = rk * k_k
    kk = kx * lax.rsqrt(_group_sum(kx * kx, RWKV_HEAD) + EPS)
    k2 = rk * (1.0 + (a - 1.0) * k_a)
    return rr, k2, rv, kk, a, log_decay, g


def _rwkv_finish(y, rr, k2, rv, g, tab):
    r_k, ln_w, ln_b = tab[4:5], tab[5:6], tab[6:7]
    inv = 1.0 / RWKV_HEAD
    mean = _group_sum(y, RWKV_HEAD) * inv
    yc = y - mean
    var = _group_sum(yc * yc, RWKV_HEAD) * inv
    yn = yc * lax.rsqrt(var + RWKV_LN_EPS) * ln_w + ln_b
    bonus = _group_sum(rr * k2 * r_k, RWKV_HEAD) * rv
    return (yn + bonus) * g


def _rwkv_chunk_heads(at, rt, bt, kt, bh, kh, v, decay_c, states):
    c = at.shape[0]
    hd = RWKV_HEAD
    heads = range(len(states))
    sls = [slice(h * hd, (h + 1) * hd) for h in heads]
    strict = _tri(c, strict=True)
    incl = _tri(c)
    mms = [_dot_nt(jnp.concatenate([at[:, sl], rt[:, sl]], axis=0),
                   jnp.concatenate([bt[:, sl], kt[:, sl]], axis=0)) for sl in sls]
    t_invs = _unit_lower_inverses([mm[:c, :c] * strict for mm in mms])
    xs = [_dot_nt(at[:, sls[h]], states[h]) + _dot(mms[h][:c, c:] * strict, v[:, sls[h]]) for h in heads]
    us = [_dot(t_invs[h], xs[h]) for h in heads]
    ys = [_dot_nt(rt[:, sls[h]], states[h]) + _dot(mms[h][c:, :c] * incl, us[h])
          + _dot(mms[h][c:, c:] * incl, v[:, sls[h]]) for h in heads]
    new = [states[h] * decay_c[:, sls[h]] + _dot_tn(us[h], bh[:, sls[h]]) + _dot_tn(v[:, sls[h]], kh[:, sls[h]])
           for h in heads]
    return ys, new


def _rwkv_prompt_kernel(r_ref, mu_ref, lw_ref, la_ref, g2_ref, tab_ref, o_ref, sout_ref, shout_ref,
                        s_ref, shift_ref, y_ref, f_ref):
    tb = r_ref.shape[0]
    c = RWKV_CHUNK
    hd = RWKV_HEAD

    @pl.when(pl.program_id(1) == 0)
    def _():
        s_ref[...] = jnp.zeros_like(s_ref)
        shift_ref[...] = jnp.zeros_like(shift_ref)

    r = r_ref[...]
    prev = jnp.where(_iota(r.shape, 0) == 0, shift_ref[...], pltpu.roll(r, 1, axis=0))
    shift_ref[...] = r[tb - 1:tb, :]
    tab = tab_ref[...]
    rr, k2, rv, kk, a, ld, g = _rwkv_prep(r, prev, mu_ref[...], lw_ref[...], la_ref[...], g2_ref[...], tab)
    f_ref[0] = rr
    f_ref[1] = k2
    f_ref[2] = rv
    f_ref[3] = kk
    f_ref[4] = a
    f_ref[5] = ld

    def chunk(ci, carry):
        rows = pl.ds(pl.multiple_of(ci * c, c), c)
        rr_c, k2_c, rv_c, kk_c, a_c, ld_c = (f_ref[i, rows, :] for i in range(6))
        gcum = _dot_3x(_tri(c), ld_c)
        g_end = gcum[c - 1:c, :]
        e_neg = jnp.exp(-gcum)
        e_end = jnp.exp(g_end - gcum)
        beta = kk_c * a_c
        at = -kk_c * jnp.exp(gcum - ld_c)
        rt = rr_c * jnp.exp(gcum)
        bt = beta * e_neg
        kt = k2_c * e_neg
        bh = beta * e_end
        kh = k2_c * e_end
        decay_c = jnp.exp(g_end)
        ys, new = _rwkv_chunk_heads(at, rt, bt, kt, bh, kh, rv_c, decay_c,
                                    [s_ref[h] for h in range(RWKV_HEADS)])
        for h in range(RWKV_HEADS):
            s_ref[h] = new[h]
            y_ref[rows, h * hd:(h + 1) * hd] = ys[h]
        return carry

    lax.fori_loop(0, tb // c, chunk, 0)
    o_ref[...] = _rwkv_finish(y_ref[...], rr, k2, rv, g, tab).astype(o_ref.dtype)
    sout_ref[0] = s_ref[...]
    shout_ref[0] = shift_ref[...]


def _rwkv_params(mu, w0, w2, a0, a2, g2, k_k, k_a, r_k, ln_w, ln_b):
    m = RWKV_MIX
    lora_w = jnp.zeros((LANES, m), F32).at[:RWKV_W_LR].set(w2)
    lora_a = jnp.zeros((LANES, m), F32).at[RWKV_W_LR:RWKV_W_LR + RWKV_A_LR].set(a2)
    tab = jnp.stack([w0, a0, k_k, k_a, r_k, ln_w, ln_b, jnp.zeros_like(w0)])
    return mu.reshape(1, RWKV_COLS), lora_w, lora_a, g2, tab


def rwkv_prompt(r_cols, params, *, batch):
    m, width = r_cols.shape
    t = m // batch
    tb = _pick_tile(t, (256, 128, 64))
    nt = t // tb
    mix = RWKV_MIX
    mu, lora_w, lora_a, g2, tab = params
    const = lambda b, i: (0, 0)
    return pl.pallas_call(
        _rwkv_prompt_kernel,
        grid=(batch, nt),
        in_specs=[pl.BlockSpec((tb, width), lambda b, i: (b * nt + i, 0)),
                  pl.BlockSpec((1, width), const),
                  pl.BlockSpec((LANES, mix), const),
                  pl.BlockSpec((LANES, mix), const),
                  pl.BlockSpec((RWKV_G_LR, mix), const),
                  pl.BlockSpec((RWKV_PARAM_ROWS, mix), const)],
        out_specs=[pl.BlockSpec((tb, mix), lambda b, i: (b * nt + i, 0)),
                   pl.BlockSpec((1, RWKV_HEADS, RWKV_HEAD, RWKV_HEAD), lambda b, i: (b, 0, 0, 0)),
                   pl.BlockSpec((1, 1, width), lambda b, i: (b, 0, 0))],
        out_shape=[jax.ShapeDtypeStruct((m, mix), BF16),
                   jax.ShapeDtypeStruct((batch, RWKV_HEADS, RWKV_HEAD, RWKV_HEAD), F32),
                   jax.ShapeDtypeStruct((batch, 1, width), F32)],
        scratch_shapes=[pltpu.VMEM((RWKV_HEADS, RWKV_HEAD, RWKV_HEAD), F32),
                        pltpu.VMEM((1, width), F32),
                        pltpu.VMEM((tb, mix), F32),
                        pltpu.VMEM((6, tb, mix), F32)],
        compiler_params=_cparams("parallel", "arbitrary"),
        name="rwkv_prompt",
    )(r_cols, mu, lora_w, lora_a, g2, tab)


DN_MIX = DN_HEADS * DN_HEAD
DN_CONV_CH = 3 * DN_MIX
DN_GATE_OFF = DN_CONV_CH + DN_MIX


def _col_to_row(col):
    n = col.shape[0]
    eye = (_iota((n, n), 0) == _iota((n, n), 1)).astype(F32)
    return jnp.sum(eye * col, axis=0, keepdims=True)


def _dn_gates(gb_tile, a_log_row, dt_bias_row):
    gdec = -jnp.exp(a_log_row) * _softplus(gb_tile + dt_bias_row)
    beta = jax.nn.sigmoid(gb_tile)
    return gdec, beta


def _l2n(x):
    return x * lax.rsqrt(jnp.sum(x * x, axis=-1, keepdims=True) + EPS)


def _dn_chunk_setup(qs, ks, vs, g_cols, beta_cols):
    c = qs[0].shape[0]
    pairs = range(len(qs))
    incl = _tri(c)
    strict = _tri(c, strict=True)
    dmats = [jnp.exp(jnp.minimum(g - _col_to_row(g), 0.0)) * incl for g in g_cols]
    kks = [_dot_nt(k, k) for k in ks]
    t_invs = _unit_lower_inverses([-(strict * beta_cols[n] * kks[n] * dmats[n]) for n in pairs])
    tvs = [_dot(t_invs[n], beta_cols[n] * vs[n]) for n in pairs]
    tks = [_dot(t_invs[n], (beta_cols[n] * jnp.exp(g_cols[n])) * ks[n]) for n in pairs]
    qks = [_dot_nt(qs[n], ks[n]) * dmats[n] for n in pairs]
    lasts = [g[c - 1:c, :] for g in g_cols]
    return [(tvs[n], tks[n], qks[n], qs[n] * jnp.exp(g_cols[n]), ks[n] * jnp.exp(lasts[n] - g_cols[n]),
             jnp.exp(lasts[n])) for n in pairs]


def _dn_chunk_scan(parts, states):
    heads = range(len(states))
    us = [parts[h][0] - _dot(parts[h][1], states[h]) for h in heads]
    outs = [_dot(parts[h][3], states[h]) + _dot(parts[h][2], us[h]) for h in heads]
    new = [parts[h][5] * states[h] + _dot_tn(parts[h][4], us[h]) for h in heads]
    return outs, new


def _shift_rows(x, k, before):
    out = pltpu.roll(x, k, axis=0)
    row = _iota(x.shape, 0)
    nb = DN_CONV - 1
    for r in range(k):
        out = jnp.where(row == r, before[nb - k + r:nb - k + r + 1, :], out)
    return out


def _dn_prompt_kernel(c_ref, cw_ref, al_ref, dtb_ref, gn_ref, o_ref, sout_ref, cout_ref,
                      s_ref, buf_ref, x_ref, g_ref, p_ref, qk_ref, dl_ref):
    tb = c_ref.shape[0]
    c = DN_CHUNK
    hd = DN_HEAD
    nb = DN_CONV - 1

    @pl.when(pl.program_id(1) == 0)
    def _():
        s_ref[...] = jnp.zeros_like(s_ref)
        buf_ref[...] = jnp.zeros_like(buf_ref)

    x = c_ref[:, :DN_CONV_CH]
    before = buf_ref[...]
    cw = cw_ref[...]
    y = x * cw[nb:nb + 1, :]
    for k in range(1, DN_CONV):
        y = y + _shift_rows(x, k, before) * cw[nb - k:nb - k + 1, :]
    buf_ref[...] = x[tb - nb:tb, :]
    xc = _silu(y)
    for h in range(DN_HEADS):
        x_ref[:, h * hd:(h + 1) * hd] = _l2n(xc[:, h * hd:(h + 1) * hd]) * hd ** -0.5
        x_ref[:, DN_MIX + h * hd:DN_MIX + (h + 1) * hd] = _l2n(xc[:, DN_MIX + h * hd:DN_MIX + (h + 1) * hd])
    x_ref[:, 2 * DN_MIX:] = xc[:, 2 * DN_MIX:]
    gdec, beta = _dn_gates(c_ref[:, DN_GATE_OFF:DN_GATE_OFF + LANES], al_ref[...], dtb_ref[...])
    g_ref[0] = gdec
    g_ref[1] = beta

    heads = range(DN_HEADS)
    n_chunks = tb // c
    group = next(n for n in (4, 2, 1) if n_chunks % n == 0)

    def setup(gi, carry):
        qs, ks, vs, gs, bs = [], [], [], [], []
        for j in range(group):
            rows = pl.ds(pl.multiple_of((gi * group + j) * c, c), c)
            gcum = _dot_3x(_tri(c), g_ref[0, rows, :])
            beta_c = g_ref[1, rows, :]
            qs += [x_ref[rows, h * hd:(h + 1) * hd] for h in heads]
            ks += [x_ref[rows, DN_MIX + h * hd:DN_MIX + (h + 1) * hd] for h in heads]
            vs += [x_ref[rows, 2 * DN_MIX + h * hd:2 * DN_MIX + (h + 1) * hd] for h in heads]
            gs += [gcum[:, h:h + 1] for h in heads]
            bs += [beta_c[:, DN_HEADS + h:DN_HEADS + h + 1] for h in heads]
        parts = _dn_chunk_setup(qs, ks, vs, gs, bs)
        for j in range(group):
            for h in heads:
                tv, tk, qk, qg, kd, dl = parts[j * DN_HEADS + h]
                ci = gi * group + j
                p_ref[0, ci, h] = tv
                p_ref[1, ci, h] = tk
                p_ref[2, ci, h] = qg
                p_ref[3, ci, h] = kd
                qk_ref[ci, h] = qk
                dl_ref[ci, h] = jnp.broadcast_to(dl, (1, hd))
        return carry

    lax.fori_loop(0, n_chunks // group, setup, 0)

    def scan(ci, carry):
        rows = pl.ds(pl.multiple_of(ci * c, c), c)
        parts = [(p_ref[0, ci, h], p_ref[1, ci, h], qk_ref[ci, h], p_ref[2, ci, h], p_ref[3, ci, h], dl_ref[ci, h])
                 for h in heads]
        outs, new = _dn_chunk_scan(parts, [s_ref[h] for h in heads])
        for h in heads:
            s_ref[h] = new[h]
            z = c_ref[rows, DN_CONV_CH + h * hd:DN_CONV_CH + (h + 1) * hd]
            o_ref[rows, h * hd:(h + 1) * hd] = (_rms(outs[h], gn_ref[...]) * _silu(z)).astype(o_ref.dtype)
        return carry

    lax.fori_loop(0, n_chunks, scan, 0)
    sout_ref[0] = s_ref[...]
    cout_ref[0] = buf_ref[...]


def _dn_params(conv_w, a_log, dt_bias, gnorm):
    al = jnp.zeros((1, LANES), F32).at[0, :DN_HEADS].set(a_log)
    dtb = jnp.zeros((1, LANES), F32).at[0, :DN_HEADS].set(dt_bias)
    return conv_w, al, dtb, gnorm.reshape(1, DN_HEAD)


def dn_prompt(d_cols, params, *, batch):
    m, width = d_cols.shape
    t = m // batch
    tb = _pick_tile(t, (256, 128, 64))
    nt = t // tb
    conv_w, al, dtb, gn = params
    const = lambda b, i: (0, 0)
    nb = DN_CONV - 1
    return pl.pallas_call(
        _dn_prompt_kernel,
        grid=(batch, nt),
        in_specs=[pl.BlockSpec((tb, width), lambda b, i: (b * nt + i, 0)),
                  pl.BlockSpec((DN_CONV, DN_CONV_CH), const),
                  pl.BlockSpec((1, LANES), const),
                  pl.BlockSpec((1, LANES), const),
                  pl.BlockSpec((1, DN_HEAD), const)],
        out_specs=[pl.BlockSpec((tb, DN_MIX), lambda b, i: (b * nt + i, 0)),
                   pl.BlockSpec((1, DN_HEADS, DN_HEAD, DN_HEAD), lambda b, i: (b, 0, 0, 0)),
                   pl.BlockSpec((1, nb, DN_CONV_CH), lambda b, i: (b, 0, 0))],
        out_shape=[jax.ShapeDtypeStruct((m, DN_MIX), BF16),
                   jax.ShapeDtypeStruct((batch, DN_HEADS, DN_HEAD, DN_HEAD), F32),
                   jax.ShapeDtypeStruct((batch, nb, DN_CONV_CH), F32)],
        scratch_shapes=[pltpu.VMEM((DN_HEADS, DN_HEAD, DN_HEAD), F32),
                        pltpu.VMEM((nb, DN_CONV_CH), F32),
                        pltpu.VMEM((tb, DN_CONV_CH), F32),
                        pltpu.VMEM((2, tb, LANES), F32),
                        pltpu.VMEM((4, tb // DN_CHUNK, DN_HEADS, DN_CHUNK, DN_HEAD), F32),
                        pltpu.VMEM((tb // DN_CHUNK, DN_HEADS, DN_CHUNK, DN_CHUNK), F32),
                        pltpu.VMEM((tb // DN_CHUNK, DN_HEADS, 1, DN_HEAD), F32)],
        compiler_params=_cparams("parallel", "arbitrary"),
        name="dn_prompt",
    )(d_cols, conv_w, al, dtb, gn)


SB_MIX = SB_HEADS * SB_HEAD
SB_PAIR = LANES // SB_HEAD


SB_SUB = LANES
SB_TQ = 256
SB_KB = 512


def _log_sigmoid_pair(z):
    lsig = jnp.minimum(z, 0.0) - jnp.log(1.0 + jnp.exp(-jnp.abs(z)))
    return lsig, lsig - z


def _sb_weights(zs, causals, runs):
    heads = range(len(zs))
    subs = range(len(zs[0]))
    tk = zs[0][0].shape[1]
    upper = (_iota((tk, tk), 0) > _iota((tk, tk), 1)).astype(BF16)
    upper2 = jnp.concatenate([upper, upper], axis=0)
    pairs = [[_log_sigmoid_pair(z) for z in zs[r]] for r in heads]
    lsig = [[p[0] for p in pairs[r]] for r in heads]
    stay = [[x[1] if causals[s] is None else jnp.where(causals[s], x[1], 0.0) for s, x in enumerate(pairs[r])]
            for r in heads]
    later = [[jnp.dot(jnp.concatenate(_split2(x), axis=1), upper2, preferred_element_type=F32)
              for x in stay[r]] for r in heads]
    sums = [[jnp.sum(x, axis=-1, keepdims=True) for x in stay[r]] for r in heads]
    carries = [[None] * len(subs) for _ in heads]
    new_runs = []
    for r in heads:
        run = runs[r]
        for s in reversed(subs):
            carries[r][s] = run
            run = run + sums[r][s]
        new_runs.append(run)
    ws = [[jnp.exp(lsig[r][s] + (later[r][s] + carries[r][s])) for s in subs] for r in heads]
    ws = [[w if causals[s] is None else jnp.where(causals[s], w, 0.0) for s, w in enumerate(ws[r])]
          for r in heads]
    return [[w.astype(BF16) for w in ws[r]] for r in heads], new_runs


def _sb_prompt_kernel(q_ref, k_ref, v_ref, b_ref, o_ref):
    tq = q_ref.shape[0]
    kb = SB_KB
    n_sub = kb // SB_SUB
    i = pl.program_id(2)
    lane = _iota((1, LANES), 1)
    masks = [(lane // SB_HEAD == r).astype(F32) for r in range(SB_PAIR)]
    q = q_ref[...] * SB_HEAD ** -0.5
    one2 = (_iota((tq, LANES), 1) < 2).astype(BF16)
    qa = [jnp.concatenate([(q * masks[r]).astype(BF16), one2], axis=1) for r in range(SB_PAIR)]
    ktail = []
    for r in range(SB_PAIR):
        b_hi, b_lo = _split2(b_ref[0, r:r + 1, :])
        sel = _iota((kb, LANES), 1)
        ktail.append(jnp.where(sel == 0, b_hi, jnp.where(sel == 1, b_lo, jnp.zeros((), BF16))))
    q_pos = i * tq + _iota((tq, SB_SUB), 0)

    def block(j, diagonal, state):
        acc, runs = state
        rows = pl.ds(pl.multiple_of(j * kb, kb), kb)
        k = k_ref[rows, :].astype(BF16)
        v = v_ref[rows, :]
        zs = []
        for r in range(SB_PAIR):
            z = _dot_nt(qa[r], jnp.concatenate([k, ktail[r]], axis=1))
            zs.append([z[:, s * SB_SUB:(s + 1) * SB_SUB] for s in range(n_sub)])
        causals = [(j * kb + s * SB_SUB + _iota((tq, SB_SUB), 1) < q_pos) if diagonal else None
                   for s in range(n_sub)]
        ws, runs = _sb_weights(zs, causals, list(runs))
        for s in range(n_sub):
            vs = v[s * SB_SUB:(s + 1) * SB_SUB, :]
            v2 = jnp.concatenate([(vs * masks[r]).astype(BF16) for r in range(SB_PAIR)], axis=0)
            acc = acc + jnp.dot(jnp.concatenate([ws[r][s] for r in range(SB_PAIR)], axis=1), v2,
                                preferred_element_type=F32)
        return acc, tuple(runs)

    init = (jnp.zeros((tq, LANES), F32), tuple(jnp.zeros((tq, 1), F32) for _ in range(SB_PAIR)))
    jd = (i * tq) // kb
    state = block(jd, True, init)
    state = lax.fori_loop(0, jd, lambda step, st: block(jd - 1 - step, False, st), state)
    o_ref[...] = state[0].astype(o_ref.dtype)


def sb_prompt(q, k, v, bias, *, batch):
    m, width = q.shape
    t = m // batch
    tq = SB_TQ
    assert t % SB_KB == 0 and SB_KB % tq == 0
    nq = t // tq
    npair = SB_HEADS // SB_PAIR
    bias_rows = jnp.broadcast_to(bias.reshape(npair, SB_PAIR, 1), (npair, SB_PAIR, LANES))
    return pl.pallas_call(
        _sb_prompt_kernel,
        grid=(batch, npair, nq),
        in_specs=[pl.BlockSpec((tq, LANES), lambda b, p, i: (b * nq + i, p)),
                  pl.BlockSpec((t, LANES), lambda b, p, i: (b, p)),
                  pl.BlockSpec((t, LANES), lambda b, p, i: (b, p)),
                  pl.BlockSpec((1, SB_PAIR, LANES), lambda b, p, i: (p, 0, 0))],
        out_specs=pl.BlockSpec((tq, LANES), lambda b, p, i: (b * nq + i, p)),
        out_shape=jax.ShapeDtypeStruct((m, width), BF16),
        compiler_params=_cparams("parallel", "parallel", "parallel"),
        name="sb_prompt",
    )(q, k, v, bias_rows)


STEP_ROWS = 8
STEP_GROUP = 2


def _gla_step_kernel(g_ref, wa_ref, ba_ref, gn_ref, s_ref, o_ref, sout_ref):
    nk = GLA_HEADS * GLA_DK
    half = GLA_HEADS * GLA_DV
    lr_off = 2 * nk + 2 * half
    lf = _log_sigmoid(_dot(g_ref[:, lr_off:lr_off + LANES], wa_ref[...]) + ba_ref[...]) * (1.0 / GLA_GATE_NORM)
    decay = jnp.exp(lf)
    for b0 in range(0, g_ref.shape[0], STEP_GROUP):
        pairs = [(b, h) for b in range(b0, b0 + STEP_GROUP) for h in range(GLA_HEADS)]
        q_cols = [_row_to_col(g_ref[b:b + 1, h * GLA_DK:(h + 1) * GLA_DK] * GLA_DK ** -0.5) for b, h in pairs]
        k_cols = [_row_to_col(g_ref[b:b + 1, nk + h * GLA_DK:nk + (h + 1) * GLA_DK]) for b, h in pairs]
        f_cols = [_row_to_col(decay[b:b + 1, h * GLA_DK:(h + 1) * GLA_DK]) for b, h in pairs]
        new = [f_cols[n] * s_ref[0, b, h] + k_cols[n] * g_ref[b:b + 1, 2 * nk + h * GLA_DV:2 * nk + (h + 1) * GLA_DV]
               for n, (b, h) in enumerate(pairs)]
        outs = [jnp.sum(q_cols[n] * new[n], axis=0, keepdims=True) for n in range(len(pairs))]
        for n, (b, h) in enumerate(pairs):
            sout_ref[b, h] = new[n]
            gate = g_ref[b:b + 1, 2 * nk + half + h * GLA_DV:2 * nk + half + (h + 1) * GLA_DV]
            o_ref[b:b + 1, h * GLA_DV:(h + 1) * GLA_DV] = _rms(outs[n], gn_ref[...]) * _silu(gate)


def gla_step(g_cols, w_a2, b_a, gnorm, state, *, layer):
    db, width = g_cols.shape
    nk = GLA_HEADS * GLA_DK
    half = GLA_HEADS * GLA_DV
    wa = jnp.zeros((LANES, nk), F32).at[:GLA_LR].set(w_a2)
    nb = STEP_ROWS
    return pl.pallas_call(
        _gla_step_kernel,
        grid=(db // nb,),
        in_specs=[pl.BlockSpec((nb, width), lambda i: (i, 0)),
                  pl.BlockSpec((LANES, nk), lambda i: (0, 0)),
                  pl.BlockSpec((1, nk), lambda i: (0, 0)),
                  pl.BlockSpec((1, GLA_DV), lambda i: (0, 0)),
                  pl.BlockSpec((1, nb, GLA_HEADS, GLA_DK, GLA_DV), lambda i: (layer, i, 0, 0, 0))],
        out_specs=[pl.BlockSpec((nb, half), lambda i: (i, 0)),
                   pl.BlockSpec((nb, GLA_HEADS, GLA_DK, GLA_DV), lambda i: (i, 0, 0, 0))],
        out_shape=[jax.ShapeDtypeStruct((db, half), F32),
                   jax.ShapeDtypeStruct((db, GLA_HEADS, GLA_DK, GLA_DV), F32)],
        compiler_params=_cparams("parallel"),
        name="gla_step",
    )(g_cols, wa, b_a.reshape(1, nk), gnorm.reshape(1, GLA_DV), state)


def _rwkv_step_kernel(r_ref, prev_ref, mu_ref, lw_ref, la_ref, g2_ref, tab_ref, s_ref, o_ref, sout_ref,
                      f_ref, ft_ref, yt_ref):
    h = pl.program_id(0)
    hd = RWKV_HEAD
    tab = tab_ref[...]

    @pl.when(h == 0)
    def _():
        rr, k2, rv, kk, a, ld, g = _rwkv_prep(r_ref[...], prev_ref[0], mu_ref[...], lw_ref[...], la_ref[...],
                                              g2_ref[...], tab)
        for n, x in enumerate((rr, k2, rv, g)):
            f_ref[n] = x
        for n, x in enumerate((-kk, jnp.exp(ld), kk * a, k2, rr, rv)):
            ft_ref[n] = x.T

    base = pl.multiple_of(h * hd, hd)
    nkk, w, kka, k2t, rt = (ft_ref[n, pl.ds(base, hd), :] for n in range(5))

    def row(i, carry):
        s = s_ref[0, 0, i]
        sa = jnp.sum(s * nkk, axis=0, keepdims=True)
        s_new = s * w + sa * kka + ft_ref[5, pl.ds(base + i, 1), :] * k2t
        sout_ref[0, i] = s_new
        yt_ref[pl.ds(base + i, 1), :] = jnp.sum(s_new * rt, axis=0, keepdims=True)
        return carry

    lax.fori_loop(0, hd, row, 0)

    @pl.when(h == pl.num_programs(0) - 1)
    def _():
        o_ref[...] = _rwkv_finish(yt_ref[...].T, f_ref[0], f_ref[1], f_ref[2], f_ref[3], tab)


def rwkv_step(r_cols, params, state, shift, *, layer):
    db, width = r_cols.shape
    mix = RWKV_MIX
    mu, lora_w, lora_a, g2, tab = params
    hd = RWKV_HEAD
    const = lambda h: (0, 0)
    y, s_t = pl.pallas_call(
        _rwkv_step_kernel,
        grid=(RWKV_HEADS,),
        in_specs=[pl.BlockSpec((db, width), const),
                  pl.BlockSpec((1, db, width), lambda h: (layer, 0, 0)),
                  pl.BlockSpec((1, width), const),
                  pl.BlockSpec((LANES, mix), const),
                  pl.BlockSpec((LANES, mix), const),
                  pl.BlockSpec((RWKV_G_LR, mix), const),
                  pl.BlockSpec((RWKV_PARAM_ROWS, mix), const),
                  pl.BlockSpec((1, 1, hd, hd, db), lambda h: (layer, h, 0, 0, 0))],
        out_specs=[pl.BlockSpec((db, mix), const),
                   pl.BlockSpec((1, hd, hd, db), lambda h: (h, 0, 0, 0))],
        out_shape=[jax.ShapeDtypeStruct((db, mix), F32),
                   jax.ShapeDtypeStruct((RWKV_HEADS, hd, hd, db), F32)],
        scratch_shapes=[pltpu.VMEM((4, db, mix), F32),
                        pltpu.VMEM((6, mix, db), F32),
                        pltpu.VMEM((mix, db), F32)],
        compiler_params=_cparams("arbitrary"),
        name="rwkv_step",
    )(r_cols, shift, mu, lora_w, lora_a, g2, tab, jnp.transpose(state, (0, 2, 3, 4, 1)))
    return y, jnp.transpose(s_t, (3, 0, 1, 2))


def _dn_step_kernel(c_ref, cw_ref, al_ref, dtb_ref, gn_ref, s_ref, buf_ref, o_ref, sout_ref, cout_ref):
    hd = DN_HEAD
    nb = DN_CONV - 1
    cw = cw_ref[...]
    gdec, beta = _dn_gates(c_ref[:, DN_GATE_OFF:DN_GATE_OFF + LANES], al_ref[...], dtb_ref[...])
    decay = jnp.exp(gdec)
    xcs = []
    for b in range(c_ref.shape[0]):
        x = c_ref[b:b + 1, :DN_CONV_CH]
        before = buf_ref[0, b]
        y = x * cw[nb:nb + 1, :]
        for k in range(nb):
            y = y + before[k:k + 1, :] * cw[k:k + 1, :]
        cout_ref[b] = jnp.concatenate([before[1:nb, :], x], axis=0)
        xcs.append(_silu(y))
    for b0 in range(0, c_ref.shape[0], STEP_GROUP):
        pairs = [(b, h) for b in range(b0, b0 + STEP_GROUP) for h in range(DN_HEADS)]
        idx = range(len(pairs))
        q_cols = [_row_to_col(_l2n(xcs[b][:, h * hd:(h + 1) * hd]) * hd ** -0.5) for b, h in pairs]
        k_cols = [_row_to_col(_l2n(xcs[b][:, DN_MIX + h * hd:DN_MIX + (h + 1) * hd])) for b, h in pairs]
        decs = [decay[b:b + 1, h:h + 1] for b, h in pairs]
        states = [s_ref[0, b, h] for b, h in pairs]
        us = [beta[b:b + 1, DN_HEADS + h:DN_HEADS + h + 1]
              * (xcs[b][:, 2 * DN_MIX + h * hd:2 * DN_MIX + (h + 1) * hd]
                 - decs[n] * jnp.sum(k_cols[n] * states[n], axis=0, keepdims=True))
              for n, (b, h) in enumerate(pairs)]
        new = [decs[n] * states[n] + k_cols[n] * us[n] for n in idx]
        outs = [jnp.sum(q_cols[n] * new[n], axis=0, keepdims=True) for n in idx]
        for n, (b, h) in enumerate(pairs):
            sout_ref[b, h] = new[n]
            z = c_ref[b:b + 1, DN_CONV_CH + h * hd:DN_CONV_CH + (h + 1) * hd]
            o_ref[b:b + 1, h * hd:(h + 1) * hd] = _rms(outs[n], gn_ref[...]) * _silu(z)


def dn_step(d_cols, params, state, conv_buf, *, layer):
    db, width = d_cols.shape
    conv_w, al, dtb, gn = params
    nb = STEP_ROWS
    nc = DN_CONV - 1
    const = lambda i: (0, 0)
    return pl.pallas_call(
        _dn_step_kernel,
        grid=(db // nb,),
        in_specs=[pl.BlockSpec((nb, width), lambda i: (i, 0)),
                  pl.BlockSpec((DN_CONV, DN_CONV_CH), const),
                  pl.BlockSpec((1, LANES), const),
                  pl.BlockSpec((1, LANES), const),
                  pl.BlockSpec((1, DN_HEAD), const),
                  pl.BlockSpec((1, nb, DN_HEADS, DN_HEAD, DN_HEAD), lambda i: (layer, i, 0, 0, 0)),
                  pl.BlockSpec((1, nb, nc, DN_CONV_CH), lambda i: (layer, i, 0, 0))],
        out_specs=[pl.BlockSpec((nb, DN_MIX), lambda i: (i, 0)),
                   pl.BlockSpec((nb, DN_HEADS, DN_HEAD, DN_HEAD), lambda i: (i, 0, 0, 0)),
                   pl.BlockSpec((nb, nc, DN_CONV_CH), lambda i: (i, 0, 0))],
        out_shape=[jax.ShapeDtypeStruct((db, DN_MIX), F32),
                   jax.ShapeDtypeStruct((db, DN_HEADS, DN_HEAD, DN_HEAD), F32),
                   jax.ShapeDtypeStruct((db, nc, DN_CONV_CH), F32)],
        compiler_params=_cparams("parallel"),
        name="dn_step",
    )(d_cols, conv_w, al, dtb, gn, state, conv_buf)


def _sb_step_kernel(pt_ref, q_ref, b_ref, *refs, n_pages):
    del pt_ref
    k_refs = refs[:n_pages]
    v_refs = refs[n_pages:2 * n_pages]
    o_ref = refs[2 * n_pages]
    width = q_ref.shape[2]
    head_mask = (_iota((SB_HEADS, width), 0) == _iota((SB_HEADS, width), 1) // SB_HEAD).astype(F32)
    qh = (q_ref[0] * SB_HEAD ** -0.5 * head_mask).astype(BF16)
    bias = b_ref[:, 0:1]
    pages = range(n_pages)
    page = k_refs[0].shape[3]
    upper = (_iota((page, page), 0) > _iota((page, page), 1)).astype(BF16)
    zs = [_dot(qh, k_refs[p][0, 0]) + bias for p in pages]
    stay = [jnp.minimum(-z, 0.0) - jnp.log(1.0 + jnp.exp(-jnp.abs(z))) for z in zs]
    parts = [_split2(x) for x in stay]
    later = [jnp.dot(hi, upper, preferred_element_type=F32) + jnp.dot(lo, upper, preferred_element_type=F32)
             for hi, lo in parts]
    sums = [jnp.sum(x, axis=-1, keepdims=True) for x in stay]
    runs = [None] * n_pages
    run = jnp.zeros((SB_HEADS, 1), F32)
    for p in reversed(pages):
        runs[p] = run
        run = run + sums[p]
    ws = [jnp.exp(zs[p] + stay[p] + (later[p] + runs[p])) for p in pages]
    acc = sum(_dot_nt(ws[p], v_refs[p][0, 0]) for p in pages)
    o_ref[0] = jnp.sum(acc * head_mask, axis=0, keepdims=True)


def sb_step(q, bias, cache_k, cache_v, page_table, *, layer):
    db, width = q.shape
    n_layers, n_phys, page = cache_k.shape[:3]
    n_pages = page_table.shape[1]
    bias_rows = jnp.broadcast_to(bias.reshape(SB_HEADS, 1), (SB_HEADS, LANES))

    def view(c):
        return jnp.transpose(c, (0, 1, 3, 4, 2)).reshape(n_layers, n_phys, width, page)

    def page_spec(p):
        return pl.BlockSpec((1, 1, width, page), lambda b, pt: (layer, pt[b, p], 0, 0))

    grid_spec = pltpu.PrefetchScalarGridSpec(
        num_scalar_prefetch=1,
        grid=(db,),
        in_specs=[pl.BlockSpec((1, 1, width), lambda b, pt: (b, 0, 0)),
                  pl.BlockSpec((SB_HEADS, LANES), lambda b, pt: (0, 0))]
                 + [page_spec(p) for p in range(n_pages)] * 2,
        out_specs=pl.BlockSpec((1, 1, width), lambda b, pt: (b, 0, 0)))
    out = pl.pallas_call(
        functools.partial(_sb_step_kernel, n_pages=n_pages),
        grid_spec=grid_spec,
        out_shape=jax.ShapeDtypeStruct((db, 1, width), F32),
        compiler_params=_cparams("parallel"),
        name="sb_step",
    )(page_table, q.reshape(db, 1, width), bias_rows,
      *([view(cache_k)] * n_pages), *([view(cache_v)] * n_pages))
    return out.reshape(db, width)


PROJ_PAD = 256


def _pad_cols(w, mult):
    n = w.shape[1]
    return jnp.pad(w, ((0, 0), (0, _round_up(n, mult) - n)))


def kernel(x_prompt, x_sample, mem_prompt, state_gla, state_rwkv, state_rwkv_shift, cache_sb_k, cache_sb_v, page_table, state_delta, state_delta_conv, cache_mem_k, cache_mem_v, norm_mix, norm_cross, norm_ffn, w_in_even, w_out_even, gla_w_a2, gla_b_a, gla_norm, rwkv_mu, rwkv_w0, rwkv_w2, rwkv_a0, rwkv_a2, rwkv_g2, rwkv_k_k, rwkv_k_a, rwkv_r_k, rwkv_ln_w, rwkv_ln_b, w_in_odd, w_out_odd, sb_q_norm, sb_k_norm, sb_bias, dn_conv, dn_a_log, dn_dt_bias, dn_norm, mem_norm, w_xq, w_xkv, w_xo, xq_norm, xk_norm, w_gu, w_down):
    b, t, d = x_prompt.shape
    db = x_sample.shape[0]
    depth = norm_mix.shape[0]
    mem_len = mem_prompt.shape[1]
    xh = d // X_HEADS
    gla_cols = 2 * GLA_HEADS * GLA_DK + 2 * GLA_HEADS * GLA_DV + GLA_LR

    xp = x_prompt.reshape(b * t, d)
    xs = x_sample.reshape(db, d)
    mem = mem_prompt.reshape(b * mem_len, d)

    p_gla, p_rwkv, p_shift, p_sbk, p_sbv, p_dn, p_conv, p_mk, p_mv = ([] for _ in range(9))
    s_gla, s_rwkv, s_shift, s_sbk, s_sbv, s_dn, s_conv = ([] for _ in range(7))

    for i in range(depth):
        j = i // 2
        if i % 2 == 0:
            w_in = w_in_even[j].astype(BF16)
            w_g = _pad_cols(w_in[:, :gla_cols], PROJ_PAD)
            w_r = w_in[:, gla_cols:]
            w_out = w_out_even[j].astype(BF16)
            wo_a, wo_b = w_out[:GLA_HEADS * GLA_DV], w_out[GLA_HEADS * GLA_DV:]
            rp = _rwkv_params(rwkv_mu[j], rwkv_w0[j], rwkv_w2[j], rwkv_a0[j], rwkv_a2[j], rwkv_g2[j],
                              rwkv_k_k[j], rwkv_k_a[j], rwkv_r_k[j], rwkv_ln_w[j], rwkv_ln_b[j])
            g_cols = norm_matmul(xp, norm_mix[i], w_g)
            r_cols = norm_matmul(xp, norm_mix[i], w_r)
            o_a, sg = gla_prompt(g_cols, gla_w_a2[j], gla_b_a[j], gla_norm[j], batch=b)
            o_b, sr, sh = rwkv_prompt(r_cols, rp, batch=b)
            xp = matmul_residual([(o_a, wo_a), (o_b, wo_b)], xp)
            p_gla.append(sg)
            p_rwkv.append(sr)
            p_shift.append(sh.reshape(b, -1))
            g_cols = norm_matmul(xs, norm_mix[i], w_g)
            r_cols = norm_matmul(xs, norm_mix[i], w_r)
            o_a, sg = gla_step(g_cols, gla_w_a2[j], gla_b_a[j], gla_norm[j], state_gla, layer=j)
            o_b, sr = rwkv_step(r_cols, rp, state_rwkv, state_rwkv_shift, layer=j)
            xs = matmul_residual([(o_a, wo_a), (o_b, wo_b)], xs)
            s_gla.append(sg)
            s_rwkv.append(sr)
            s_shift.append(r_cols)
        else:
            w_in = w_in_odd[j].astype(BF16)
            w_q, w_k, w_v = (w_in[:, n * SB_MIX:(n + 1) * SB_MIX] for n in range(3))
            w_d = _pad_cols(w_in[:, 3 * SB_MIX:], PROJ_PAD)
            w_out = w_out_odd[j].astype(BF16)
            wo_a, wo_b = w_out[:SB_MIX], w_out[SB_MIX:]
            qn = jnp.tile(sb_q_norm[j], SB_HEADS)
            kn = jnp.tile(sb_k_norm[j], SB_HEADS)
            dp = _dn_params(dn_conv[j], dn_a_log[j], dn_dt_bias[j], dn_norm[j])
            q = norm_matmul(xp, norm_mix[i], w_q, head_gain=qn, head_group=SB_HEAD)
            k = norm_matmul(xp, norm_mix[i], w_k, head_gain=kn, head_group=SB_HEAD)
            v = norm_matmul(xp, norm_mix[i], w_v)
            d_cols = norm_matmul(xp, norm_mix[i], w_d)
            o_a = sb_prompt(q, k, v, sb_bias[j], batch=b)
            o_b, sd, sc = dn_prompt(d_cols, dp, batch=b)
            xp = matmul_residual([(o_a, wo_a), (o_b, wo_b)], xp)
            p_sbk.append(k.reshape(b, t, SB_HEADS, SB_HEAD))
            p_sbv.append(v.reshape(b, t, SB_HEADS, SB_HEAD))
            p_dn.append(sd)
            p_conv.append(sc)
            q = norm_matmul(xs, norm_mix[i], w_q, head_gain=qn, head_group=SB_HEAD)
            k = norm_matmul(xs, norm_mix[i], w_k, head_gain=kn, head_group=SB_HEAD)
            v = norm_matmul(xs, norm_mix[i], w_v)
            d_cols = norm_matmul(xs, norm_mix[i], w_d)
            o_a = sb_step(q, sb_bias[j], cache_sb_k, cache_sb_v, page_table, layer=j)
            o_b, sd, sc = dn_step(d_cols, dp, state_delta, state_delta_conv, layer=j)
            xs = matmul_residual([(o_a, wo_a), (o_b, wo_b)], xs)
            s_sbk.append(k.reshape(db, 1, SB_HEADS, SB_HEAD))
            s_sbv.append(v.reshape(db, 1, SB_HEADS, SB_HEAD))
            s_dn.append(sd)
            s_conv.append(sc)

        w_kv = w_xkv[i].astype(BF16)
        wq = w_xq[i].astype(BF16)
        wo = w_xo[i].astype(BF16)
        mk = norm_matmul(mem, mem_norm[i], w_kv[:, :d], head_gain=jnp.tile(xk_norm[i], X_HEADS), head_group=xh)
        mv = norm_matmul(mem, mem_norm[i], w_kv[:, d:])
        p_mk.append(mk.reshape(b, mem_len, X_HEADS, xh))
        p_mv.append(mv.reshape(b, mem_len, X_HEADS, xh))
        xp = cross_attn_prompt(xp, norm_cross[i], wq, xq_norm[i], mk, mv, wo, batch=b)
        q = norm_matmul(xs, norm_cross[i], wq, head_gain=jnp.tile(xq_norm[i], X_HEADS), head_group=xh)
        xs = matmul_residual([(cross_attn_sample(q, cache_mem_k, cache_mem_v, layer=i), wo)], xs)

        wgu = w_gu[i].astype(BF16)
        wdn = w_down[i].astype(BF16)
        xp = ffn(xp, norm_ffn[i], wgu, wdn)
        xs = ffn(xs, norm_ffn[i], wgu, wdn)

    return (xp.reshape(b, t, d), xs.reshape(db, 1, d),
            jnp.stack(p_gla), jnp.stack(p_rwkv), jnp.stack(p_shift),
            jnp.stack(p_sbk), jnp.stack(p_sbv), jnp.stack(p_dn), jnp.stack(p_conv),
            jnp.stack(p_mk), jnp.stack(p_mv),
            jnp.stack(s_gla), jnp.stack(s_rwkv), jnp.stack(s_shift),
            jnp.stack(s_sbk), jnp.stack(s_sbv), jnp.stack(s_dn), jnp.stack(s_conv))
```
